```python
import math
import jax, jax.numpy as jnp
from jax import lax
import numpy as np

D_MODEL = 2048
BATCH = 4
SEQ = 4096
DEPTH = 4

GRID_W = 64
HEAD_DIM = 128
N_Q_HEADS = 16
N_KV_HEADS = 4
GROUP = N_Q_HEADS // N_KV_HEADS
ATTN_WIDTH = N_Q_HEADS * HEAD_DIM
KV_WIDTH = N_KV_HEADS * HEAD_DIM
CONV_WIDTH = D_MODEL
CONV_K = 3
D_FF = 4 * D_MODEL
Q_BLOCK = 128
ROPE_THETA = 10000.0
RMS_EPS = 1e-6
AXIS_DIM = HEAD_DIM // 2
N_FREQ = AXIS_DIM // 2
IN_SPLITS = (CONV_WIDTH, CONV_WIDTH, CONV_WIDTH, ATTN_WIDTH, KV_WIDTH, KV_WIDTH, D_MODEL, D_MODEL)
IN_WIDTH = sum(IN_SPLITS)
IN_OFFSETS = tuple(int(o) for o in np.cumsum(IN_SPLITS)[:-1])

kernel_name = 'hybrid_shortconv_gqa_axial_encoder'


def rmsnorm(x, g):
    xf = x.astype(jnp.float32)
    y = xf * lax.rsqrt(jnp.mean(xf * xf, axis=-1, keepdims=True) + RMS_EPS)
    return (y * g.astype(jnp.float32)).astype(x.dtype)


def axial_rope_tables(seq_len):
    rows = seq_len // GRID_W
    row_idx = jnp.repeat(jnp.arange(rows, dtype=jnp.int32), GRID_W)
    col_idx = jnp.tile(jnp.arange(GRID_W, dtype=jnp.int32), rows)
    inv_freq = ROPE_THETA ** (-jnp.arange(0, AXIS_DIM, 2, dtype=jnp.float32) / AXIS_DIM)
    ang = jnp.stack([row_idx.astype(jnp.float32)[:, None] * inv_freq,
                     col_idx.astype(jnp.float32)[:, None] * inv_freq], axis=1)
    return jnp.cos(ang), jnp.sin(ang)


def apply_axial_rope(x, cos, sin):
    b, s, h, _ = x.shape
    xr = x.astype(jnp.float32).reshape(b, s, h, 2, 2, N_FREQ)
    x1, x2 = xr[..., 0, :], xr[..., 1, :]
    c, sn = cos[None, :, None], sin[None, :, None]
    out = jnp.stack([x1 * c - x2 * sn, x2 * c + x1 * sn], axis=-2)
    return out.reshape(b, s, h, HEAD_DIM).astype(x.dtype)


def short_conv_mixer(conv_b, conv_c, h_in, w_conv, w_out):
    u = conv_c * h_in
    up = jnp.pad(u, ((0, 0), (1, 1), (0, 0)))
    conv = w_conv[0] * up[:, :-2] + w_conv[1] * up[:, 1:-1] + w_conv[2] * up[:, 2:]
    return (conv_b * conv) @ w_out


def block_gqa(q, k, v):
    b, s, _, _ = q.shape
    nb = s // Q_BLOCK
    scale = 1.0 / math.sqrt(HEAD_DIM)
    qb = (q * scale).reshape(b, nb, Q_BLOCK, N_KV_HEADS, GROUP, HEAD_DIM).transpose(1, 0, 2, 3, 4, 5)

    def one_block(q_blk):
        scores = jnp.einsum('bqkgd,bskd->bkgqs', q_blk, k).astype(jnp.float32)
        p = jax.nn.softmax(scores, axis=-1).astype(v.dtype)
        return jnp.einsum('bkgqs,bskd->bqkgd', p, v)

    o = lax.map(one_block, qb)
    return o.transpose(1, 0, 2, 3, 4, 5).reshape(b, s, ATTN_WIDTH)


def setup_inputs(seed: int = 0) -> dict:
    key = jax.random.key(seed)
    ks = jax.random.split(key, 16)
    f32 = jnp.float32
    nrm = lambda k, shape, scale: jax.random.normal(k, shape, f32) * scale
    gain = lambda k, shape: 1.0 + 0.02 * jax.random.normal(k, shape, f32)
    return {
        'x': jax.random.normal(ks[0], (BATCH, SEQ, D_MODEL), f32),
        'norm_mix_pre': gain(ks[1], (DEPTH, D_MODEL)),
        'w_in': nrm(ks[2], (DEPTH, D_MODEL, IN_WIDTH), D_MODEL ** -0.5),
        'gate_bias': nrm(ks[3], (DEPTH, 2 * D_MODEL), 0.01),
        'conv_w': nrm(ks[4], (DEPTH, CONV_K, CONV_WIDTH), CONV_K ** -0.5),
        'q_norm': gain(ks[5], (DEPTH, HEAD_DIM)),
        'k_norm': gain(ks[6], (DEPTH, HEAD_DIM)),
        'w_out_conv': nrm(ks[7], (DEPTH, CONV_WIDTH, D_MODEL), CONV_WIDTH ** -0.5),
        'w_out_attn': nrm(ks[8], (DEPTH, ATTN_WIDTH, D_MODEL), ATTN_WIDTH ** -0.5),
        'w_merge': nrm(ks[9], (DEPTH, D_MODEL, D_MODEL), D_MODEL ** -0.5),
        'norm_mix_post': gain(ks[10], (DEPTH, D_MODEL)),
        'norm_mlp_pre': gain(ks[11], (DEPTH, D_MODEL)),
        'w_up': nrm(ks[12], (DEPTH, D_MODEL, D_FF), D_MODEL ** -0.5),
        'w_down': nrm(ks[13], (DEPTH, D_FF, D_MODEL), D_FF ** -0.5),
        'norm_mlp_post': gain(ks[14], (DEPTH, D_MODEL)),
    }


def reference(x, norm_mix_pre, w_in, gate_bias, conv_w, q_norm, k_norm, w_out_conv, w_out_attn,
              w_merge, norm_mix_post, norm_mlp_pre, w_up, w_down, norm_mlp_post):
    b, s, _ = x.shape
    cos, sin = axial_rope_tables(s)
    for l in range(DEPTH):
        h = rmsnorm(x, norm_mix_pre[l])
        z = h @ w_in[l]
        conv_b, conv_c, conv_in, q, k, v, g_a, g_b = jnp.split(z, IN_OFFSETS, axis=-1)

        y_a = short_conv_mixer(conv_b, conv_c, conv_in, conv_w[l], w_out_conv[l])

        q = apply_axial_rope(rmsnorm(q.reshape(b, s, N_Q_HEADS, HEAD_DIM), q_norm[l]), cos, sin)
        k = apply_axial_rope(rmsnorm(k.reshape(b, s, N_KV_HEADS, HEAD_DIM), k_norm[l]), cos, sin)
        v = v.reshape(b, s, N_KV_HEADS, HEAD_DIM)
        y_b = block_gqa(q, k, v) @ w_out_attn[l]

        gates = jax.nn.sigmoid(jnp.concatenate([g_a, g_b], axis=-1) + gate_bias[l])
        gate_a, gate_b = jnp.split(gates, 2, axis=-1)
        mixed = (gate_a * y_a + gate_b * y_b) @ w_merge[l]
        x = x + rmsnorm(mixed, norm_mix_post[l])

        h = rmsnorm(x, norm_mlp_pre[l])
        f = jnp.square(jax.nn.relu(h @ w_up[l])) @ w_down[l]
        x = x + rmsnorm(f, norm_mlp_post[l])
    return x
```

```python
import functools
import math

import jax
import jax.numpy as jnp
import numpy as np
from jax.experimental import pallas as pl
from jax.experimental.pallas import tpu as pltpu

D_MODEL = 2048
GRID_W = 64
HEAD_DIM = 128
N_Q_HEADS = 16
N_KV_HEADS = 4
GROUP = N_Q_HEADS // N_KV_HEADS
ATTN_WIDTH = N_Q_HEADS * HEAD_DIM
KV_WIDTH = N_KV_HEADS * HEAD_DIM
CONV_WIDTH = D_MODEL
D_FF = 4 * D_MODEL
ROPE_THETA = 10000.0
RMS_EPS = 1e-6
AXIS_DIM = HEAD_DIM // 2
N_FREQ = AXIS_DIM // 2
IN_WIDTH = 3 * CONV_WIDTH + ATTN_WIDTH + 2 * KV_WIDTH + 2 * D_MODEL

Z_B, Z_C, Z_IN, Z_Q, Z_GA, Z_GB = (i * D_MODEL for i in range(6))
Z_K = 6 * D_MODEL
Z_V = Z_K + KV_WIDTH

V7X_VMEM_BYTES = 64 * 1024 * 1024
VMEM_LIMIT = V7X_VMEM_BYTES - 8 * 1024 * 1024

BF16 = jnp.bfloat16
F32 = jnp.float32


def _params(n_axes):
    return pltpu.CompilerParams(
        dimension_semantics=("arbitrary",) * n_axes, vmem_limit_bytes=VMEM_LIMIT)


def _rms_scale(xf):
    return jax.lax.rsqrt(jnp.mean(xf * xf, axis=-1, keepdims=True) + RMS_EPS)


def _in_proj_kernel(x_ref, g_ref, w_ref, z_ref, h_ref):
    @pl.when(pl.program_id(1) == 0)
    def _():
        xf = x_ref[...]
        h_ref[...] = (xf * _rms_scale(xf) * g_ref[...]).astype(BF16)

    z_ref[...] = jnp.dot(h_ref[...], w_ref[...],
                         preferred_element_type=F32).astype(BF16)


def _in_proj(x2, gain, w_in, layer, tm=1024, tn=1024):
    t = x2.shape[0]
    return pl.pallas_call(
        _in_proj_kernel,
        grid=(t // tm, IN_WIDTH // tn),
        in_specs=[
            pl.BlockSpec((tm, D_MODEL), lambda i, j: (i, 0)),
            pl.BlockSpec((None, 1, D_MODEL), lambda i, j: (layer, 0, 0)),
            pl.BlockSpec((None, D_MODEL, tn), lambda i, j: (layer, 0, j)),
        ],
        out_specs=pl.BlockSpec((tm, tn), lambda i, j: (i, j)),
        out_shape=jax.ShapeDtypeStruct((t, IN_WIDTH), BF16),
        scratch_shapes=[pltpu.VMEM((tm, D_MODEL), BF16)],
        compiler_params=_params(2),
        name="in_proj",
    )(x2, gain, w_in)


def _conv_kernel(b_ref, c_ref, i_ref, w_ref, o_ref):
    s = b_ref.shape[0]
    u = c_ref[...].astype(F32) * i_ref[...].astype(F32)
    row = jax.lax.broadcasted_iota(jnp.int32, u.shape, 0)
    prev = jnp.where(row == 0, 0.0, pltpu.roll(u, 1, 0))
    nxt = jnp.where(row == s - 1, 0.0, pltpu.roll(u, s - 1, 0))
    w = w_ref[...]
    conv = w[0:1] * prev + w[1:2] * u + w[2:3] * nxt
    o_ref[...] = (b_ref[...].astype(F32) * conv).astype(BF16)


def _conv_mix(z3, conv_w, layer, tc=256):
    b, s, _ = z3.shape
    zspec = lambda off: pl.BlockSpec(
        (None, s, tc), lambda bi, j, off=off: (bi, 0, off // tc + j))
    return pl.pallas_call(
        _conv_kernel,
        grid=(b, CONV_WIDTH // tc),
        in_specs=[
            zspec(Z_B), zspec(Z_C), zspec(Z_IN),
            pl.BlockSpec((None, 3, tc), lambda bi, j: (layer, 0, j)),
        ],
        out_specs=pl.BlockSpec((None, s, tc), lambda bi, j: (bi, 0, j)),
        out_shape=jax.ShapeDtypeStruct((b, s, CONV_WIDTH), BF16),
        compiler_params=_params(2),
        name="conv_mix",
    )(z3, z3, z3, conv_w)


def _rope_head(x, gain, cos, sin_signed):
    lane = jax.lax.broadcasted_iota(jnp.int32, x.shape, 1)
    y = x * _rms_scale(x) * gain
    first_half = (lane % AXIS_DIM) < N_FREQ
    partner = jnp.where(first_half,
                        pltpu.roll(y, HEAD_DIM - N_FREQ, 1),
                        pltpu.roll(y, N_FREQ, 1))
    return y * cos + partner * sin_signed


def _qk_prep_kernel(q_ref, k_ref, qg_ref, kg_ref, cos_ref, sin_ref, qo_ref, ko_ref):
    cos = cos_ref[...]
    sin = sin_ref[...]
    scale = 1.0 / math.sqrt(HEAD_DIM)
    for h in range(N_Q_HEADS):
        sl = slice(h * HEAD_DIM, (h + 1) * HEAD_DIM)
        y = _rope_head(q_ref[:, sl].astype(F32), qg_ref[...], cos, sin)
        qo_ref[:, sl] = (y * scale).astype(BF16)
    for h in range(N_KV_HEADS):
        sl = slice(h * HEAD_DIM, (h + 1) * HEAD_DIM)
        y = _rope_head(k_ref[:, sl].astype(F32), kg_ref[...], cos, sin)
        ko_ref[:, sl] = y.astype(BF16)


def _qk_prep(z2, q_gain, k_gain, cos, sin, layer, seq, tm=512):
    t = z2.shape[0]
    nseq = seq // tm
    return pl.pallas_call(
        _qk_prep_kernel,
        grid=(t // tm,),
        in_specs=[
            pl.BlockSpec((tm, ATTN_WIDTH), lambda i: (i, Z_Q // ATTN_WIDTH)),
            pl.BlockSpec((tm, KV_WIDTH), lambda i: (i, Z_K // KV_WIDTH)),
            pl.BlockSpec((None, 1, HEAD_DIM), lambda i: (layer, 0, 0)),
            pl.BlockSpec((None, 1, HEAD_DIM), lambda i: (layer, 0, 0)),
            pl.BlockSpec((tm, HEAD_DIM), lambda i: (i % nseq, 0)),
            pl.BlockSpec((tm, HEAD_DIM), lambda i: (i % nseq, 0)),
        ],
        out_specs=[
            pl.BlockSpec((tm, ATTN_WIDTH), lambda i: (i, 0)),
            pl.BlockSpec((tm, KV_WIDTH), lambda i: (i, 0)),
        ],
        out_shape=[
            jax.ShapeDtypeStruct((t, ATTN_WIDTH), BF16),
            jax.ShapeDtypeStruct((t, KV_WIDTH), BF16),
        ],
        compiler_params=_params(1),
        name="qk_prep",
    )(z2, z2, q_gain, k_gain, cos, sin)


def _attn_kernel(q_ref, k_ref, v_ref, o_ref):
    tq = q_ref.shape[0]
    q = jnp.concatenate(
        [q_ref[:, j * HEAD_DIM:(j + 1) * HEAD_DIM] for j in range(GROUP)], axis=0)
    s = jax.lax.dot_general(q, k_ref[...], (((1,), (1,)), ((), ())),
                            preferred_element_type=F32)
    m = jnp.max(s, axis=-1, keepdims=True)
    p = jnp.exp(s - m)
    l = jnp.sum(p, axis=-1, keepdims=True)
    o = jnp.dot(p.astype(BF16), v_ref[...], preferred_element_type=F32)
    o = (o / l).astype(BF16)
    for j in range(GROUP):
        o_ref[:, j * HEAD_DIM:(j + 1) * HEAD_DIM] = o[j * tq:(j + 1) * tq]


def _attention(qn, kn, z2, batch, seq, tq=128):
    t = qn.shape[0]
    nq = seq // tq
    gw = GROUP * HEAD_DIM
    return pl.pallas_call(
        _attn_kernel,
        grid=(batch, N_KV_HEADS, nq),
        in_specs=[
            pl.BlockSpec((tq, gw), lambda b, g, i: (b * nq + i, g)),
            pl.BlockSpec((seq, HEAD_DIM), lambda b, g, i: (b, g)),
            pl.BlockSpec((seq, HEAD_DIM), lambda b, g, i: (b, Z_V // HEAD_DIM + g)),
        ],
        out_specs=pl.BlockSpec((tq, gw), lambda b, g, i: (b * nq + i, g)),
        out_shape=jax.ShapeDtypeStruct((t, ATTN_WIDTH), BF16),
        compiler_params=_params(3),
        name="attention",
    )(qn, kn, z2)


def _gate_kernel(a_ref, o_ref, ga_ref, gb_ref, ba_ref, bb_ref, wc_ref, wa_ref, m_ref):
    ya = jnp.dot(a_ref[...], wc_ref[...], preferred_element_type=F32)
    yb = jnp.dot(o_ref[...], wa_ref[...], preferred_element_type=F32)
    gate_a = jax.nn.sigmoid(ga_ref[...].astype(F32) + ba_ref[...])
    gate_b = jax.nn.sigmoid(gb_ref[...].astype(F32) + bb_ref[...])
    m_ref[...] = (gate_a * ya + gate_b * yb).astype(BF16)


def _gated_branches(a2, o2, z2, bias_a, bias_b, w_conv_out, w_attn_out, layer,
                    tm=1024, tn=1024):
    t = a2.shape[0]
    row_spec = pl.BlockSpec((tm, D_MODEL), lambda i, j: (i, 0))
    zspec = lambda off: pl.BlockSpec((tm, tn), lambda i, j, off=off: (i, off // tn + j))
    bspec = pl.BlockSpec((None, 1, tn), lambda i, j: (layer, 0, j))
    wspec = pl.BlockSpec((None, D_MODEL, tn), lambda i, j: (layer, 0, j))
    return pl.pallas_call(
        _gate_kernel,
        grid=(t // tm, D_MODEL // tn),
        in_specs=[row_spec, row_spec, zspec(Z_GA), zspec(Z_GB), bspec, bspec,
                  wspec, wspec],
        out_specs=pl.BlockSpec((tm, tn), lambda i, j: (i, j)),
        out_shape=jax.ShapeDtypeStruct((t, D_MODEL), BF16),
        compiler_params=_params(2),
        name="gated_branches",
    )(a2, o2, z2, z2, bias_a, bias_b, w_conv_out, w_attn_out)


def _proj_norm_res_kernel(m_ref, w_ref, g_ref, x_ref, o_ref):
    y = jnp.dot(m_ref[...], w_ref[...], preferred_element_type=F32)
    o_ref[...] = x_ref[...] + y * _rms_scale(y) * g_ref[...]


def _proj_norm_res(m2, w, gain, x2, layer, tm=512):
    t = m2.shape[0]
    return pl.pallas_call(
        _proj_norm_res_kernel,
        grid=(t // tm,),
        in_specs=[
            pl.BlockSpec((tm, D_MODEL), lambda i: (i, 0)),
            pl.BlockSpec((None, D_MODEL, D_MODEL), lambda i: (layer, 0, 0)),
            pl.BlockSpec((None, 1, D_MODEL), lambda i: (layer, 0, 0)),
            pl.BlockSpec((tm, D_MODEL), lambda i: (i, 0)),
        ],
        out_specs=pl.BlockSpec((tm, D_MODEL), lambda i: (i, 0)),
        out_shape=jax.ShapeDtypeStruct((t, D_MODEL), F32),
        compiler_params=_params(1),
        name="merge_proj",
    )(m2, w, gain, x2)


def _mlp_kernel(x_ref, gpre_ref, wu_ref, wd_ref, gpost_ref, o_ref, h_ref, acc_ref):
    k = pl.program_id(1)

    @pl.when(k == 0)
    def _():
        xf = x_ref[...]
        h_ref[...] = (xf * _rms_scale(xf) * gpre_ref[...]).astype(BF16)

    up = jnp.dot(h_ref[...], wu_ref[...], preferred_element_type=F32)
    act = jnp.square(jnp.maximum(up, 0.0)).astype(BF16)
    part = jnp.dot(act, wd_ref[...], preferred_element_type=F32)

    @pl.when(k == 0)
    def _():
        acc_ref[...] = part

    @pl.when(k > 0)
    def _():
        acc_ref[...] += part

    @pl.when(k == pl.num_programs(1) - 1)
    def _():
        f = acc_ref[...]
        o_ref[...] = x_ref[...] + f * _rms_scale(f) * gpost_ref[...]


def _mlp(x2, g_pre, w_up, w_down, g_post, layer, tm=512, tf=1024):
    t = x2.shape[0]
    gspec = pl.BlockSpec((None, 1, D_MODEL), lambda i, k: (layer, 0, 0))
    return pl.pallas_call(
        _mlp_kernel,
        grid=(t // tm, D_FF // tf),
        in_specs=[
            pl.BlockSpec((tm, D_MODEL), lambda i, k: (i, 0)),
            gspec,
            pl.BlockSpec((None, D_MODEL, tf), lambda i, k: (layer, 0, k)),
            pl.BlockSpec((None, tf, D_MODEL), lambda i, k: (layer, k, 0)),
            gspec,
        ],
        out_specs=pl.BlockSpec((tm, D_MODEL), lambda i, k: (i, 0)),
        out_shape=jax.ShapeDtypeStruct((t, D_MODEL), F32),
        scratch_shapes=[pltpu.VMEM((tm, D_MODEL), BF16),
                        pltpu.VMEM((tm, D_MODEL), F32)],
        compiler_params=_params(2),
        name="mlp",
    )(x2, g_pre, w_up, w_down, g_post)


def _rope_tables(seq):
    pos = np.arange(seq)
    inv_freq = ROPE_THETA ** (-np.arange(0, AXIS_DIM, 2, dtype=np.float32) / AXIS_DIM)
    inv_freq = jnp.asarray(inv_freq, F32)
    row = jnp.asarray(pos // GRID_W, F32)[:, None] * inv_freq
    col = jnp.asarray(pos % GRID_W, F32)[:, None] * inv_freq
    cos = jnp.concatenate([jnp.cos(row), jnp.cos(row), jnp.cos(col), jnp.cos(col)], axis=1)
    sin = jnp.concatenate([-jnp.sin(row), jnp.sin(row), -jnp.sin(col), jnp.sin(col)], axis=1)
    return cos, sin


def kernel(x, norm_mix_pre, w_in, gate_bias, conv_w, q_norm, k_norm, w_out_conv, w_out_attn,
           w_merge, norm_mix_post, norm_mlp_pre, w_up, w_down, norm_mlp_post):
    b, s, d = x.shape
    depth = w_in.shape[0]
    t = b * s

    o_k = 3 * CONV_WIDTH + ATTN_WIDTH
    o_g = o_k + 2 * KV_WIDTH
    w_in_b = jnp.concatenate(
        [w_in[:, :, :o_k], w_in[:, :, o_g:], w_in[:, :, o_k:o_g]], axis=-1).astype(BF16)
    w_conv_out_b = w_out_conv.astype(BF16)
    w_attn_out_b = w_out_attn.astype(BF16)
    w_merge_b = w_merge.astype(BF16)
    w_up_b = w_up.astype(BF16)
    w_down_b = w_down.astype(BF16)
    row3 = lambda p: p.reshape(depth, 1, p.shape[-1])
    bias_a = row3(gate_bias[:, :D_MODEL])
    bias_b = row3(gate_bias[:, D_MODEL:])
    cos, sin = _rope_tables(s)

    x2 = x.reshape(t, d)
    for l in range(depth):
        z2 = _in_proj(x2, row3(norm_mix_pre), w_in_b, l)
        a3 = _conv_mix(z2.reshape(b, s, IN_WIDTH), conv_w, l)
        qn, kn = _qk_prep(z2, row3(q_norm), row3(k_norm), cos, sin, l, s)
        o2 = _attention(qn, kn, z2, b, s)
        m2 = _gated_branches(a3.reshape(t, CONV_WIDTH), o2, z2, bias_a, bias_b,
                             w_conv_out_b, w_attn_out_b, l)
        x2 = _proj_norm_res(m2, w_merge_b, row3(norm_mix_post), x2, l)
        x2 = _mlp(x2, row3(norm_mlp_pre), w_up_b, w_down_b, row3(norm_mlp_post), l)
    return x2.reshape(b, s, d)
```

```python
import functools
import math

import jax
import jax.numpy as jnp
import numpy as np
from jax.experimental import pallas as pl
from jax.experimental.pallas import tpu as pltpu

D_MODEL = 2048
GRID_W = 64
HEAD_DIM = 128
N_Q_HEADS = 16
N_KV_HEADS = 4
GROUP = N_Q_HEADS // N_KV_HEADS
ATTN_WIDTH = N_Q_HEADS * HEAD_DIM
KV_WIDTH = N_KV_HEADS * HEAD_DIM
CONV_WIDTH = D_MODEL
D_FF = 4 * D_MODEL
ROPE_THETA = 10000.0
RMS_EPS = 1e-6
AXIS_DIM = HEAD_DIM // 2
N_FREQ = AXIS_DIM // 2
IN_WIDTH = 3 * CONV_WIDTH + ATTN_WIDTH + 2 * KV_WIDTH + 2 * D_MODEL

Z_B, Z_C, Z_IN, Z_Q, Z_GA, Z_GB = (i * D_MODEL for i in range(6))
Z_K = 6 * D_MODEL
Z_V = Z_K + KV_WIDTH

V7X_VMEM_BYTES = 64 * 1024 * 1024
VMEM_LIMIT = V7X_VMEM_BYTES - 8 * 1024 * 1024

BF16 = jnp.bfloat16
F32 = jnp.float32


def _params(n_axes):
    return pltpu.CompilerParams(
        dimension_semantics=("arbitrary",) * n_axes, vmem_limit_bytes=VMEM_LIMIT)


def _rms_scale(xf):
    return jax.lax.rsqrt(jnp.mean(xf * xf, axis=-1, keepdims=True) + RMS_EPS)


def _in_proj_kernel(x_ref, g_ref, w_ref, z_ref, h_ref):
    @pl.when(pl.program_id(1) == 0)
    def _():
        xf = x_ref[...]
        h_ref[...] = (xf * _rms_scale(xf) * g_ref[...]).astype(BF16)

    z_ref[...] = jnp.dot(h_ref[...], w_ref[...],
                         preferred_element_type=F32).astype(BF16)


def _in_proj(x2, gain, w_in, layer, tm=1024, tn=1024):
    t = x2.shape[0]
    return pl.pallas_call(
        _in_proj_kernel,
        grid=(t // tm, IN_WIDTH // tn),
        in_specs=[
            pl.BlockSpec((tm, D_MODEL), lambda i, j: (i, 0)),
            pl.BlockSpec((None, 1, D_MODEL), lambda i, j: (layer, 0, 0)),
            pl.BlockSpec((None, D_MODEL, tn), lambda i, j: (layer, 0, j)),
        ],
        out_specs=pl.BlockSpec((tm, tn), lambda i, j: (i, j)),
        out_shape=jax.ShapeDtypeStruct((t, IN_WIDTH), BF16),
        scratch_shapes=[pltpu.VMEM((tm, D_MODEL), BF16)],
        compiler_params=_params(2),
        name="in_proj",
    )(x2, gain, w_in)


def _conv_kernel(b_ref, c_ref, i_ref, w_ref, o_ref):
    s = b_ref.shape[0]
    u = c_ref[...].astype(F32) * i_ref[...].astype(F32)
    row = jax.lax.broadcasted_iota(jnp.int32, u.shape, 0)
    prev = jnp.where(row == 0, 0.0, pltpu.roll(u, 1, 0))
    nxt = jnp.where(row == s - 1, 0.0, pltpu.roll(u, s - 1, 0))
    w = w_ref[...]
    conv = w[0:1] * prev + w[1:2] * u + w[2:3] * nxt
    o_ref[...] = (b_ref[...].astype(F32) * conv).astype(BF16)


def _conv_mix(z3, conv_w, layer, tc=256):
    b, s, _ = z3.shape
    zspec = lambda off: pl.BlockSpec(
        (None, s, tc), lambda bi, j, off=off: (bi, 0, off // tc + j))
    return pl.pallas_call(
        _conv_kernel,
        grid=(b, CONV_WIDTH // tc),
        in_specs=[
            zspec(Z_B), zspec(Z_C), zspec(Z_IN),
            pl.BlockSpec((None, 3, tc), lambda bi, j: (layer, 0, j)),
        ],
        out_specs=pl.BlockSpec((None, s, tc), lambda bi, j: (bi, 0, j)),
        out_shape=jax.ShapeDtypeStruct((b, s, CONV_WIDTH), BF16),
        compiler_params=_params(2),
        name="conv_mix",
    )(z3, z3, z3, conv_w)


def _rope_head(x, gain, cos, sin_signed):
    lane = jax.lax.broadcasted_iota(jnp.int32, x.shape, 1)
    y = x * _rms_scale(x) * gain
    first_half = (lane % AXIS_DIM) < N_FREQ
    partner = jnp.where(first_half,
                        pltpu.roll(y, HEAD_DIM - N_FREQ, 1),
                        pltpu.roll(y, N_FREQ, 1))
    return y * cos + partner * sin_signed


def _qk_prep_kernel(q_ref, k_ref, qg_ref, kg_ref, cos_ref, sin_ref, qo_ref, ko_ref):
    cos = cos_ref[...]
    sin = sin_ref[...]
    scale = math.log2(math.e) / math.sqrt(HEAD_DIM)
    for h in range(N_Q_HEADS):
        sl = slice(h * HEAD_DIM, (h + 1) * HEAD_DIM)
        y = _rope_head(q_ref[:, sl].astype(F32), qg_ref[...], cos, sin)
        qo_ref[:, sl] = (y * scale).astype(BF16)
    for h in range(N_KV_HEADS):
        sl = slice(h * HEAD_DIM, (h + 1) * HEAD_DIM)
        y = _rope_head(k_ref[:, sl].astype(F32), kg_ref[...], cos, sin)
        ko_ref[:, sl] = y.astype(BF16)


def _qk_prep(z2, q_gain, k_gain, cos, sin, layer, seq, tm=512):
    t = z2.shape[0]
    nseq = seq // tm
    return pl.pallas_call(
        _qk_prep_kernel,
        grid=(t // tm,),
        in_specs=[
            pl.BlockSpec((tm, ATTN_WIDTH), lambda i: (i, Z_Q // ATTN_WIDTH)),
            pl.BlockSpec((tm, KV_WIDTH), lambda i: (i, Z_K // KV_WIDTH)),
            pl.BlockSpec((None, 1, HEAD_DIM), lambda i: (layer, 0, 0)),
            pl.BlockSpec((None, 1, HEAD_DIM), lambda i: (layer, 0, 0)),
            pl.BlockSpec((tm, HEAD_DIM), lambda i: (i % nseq, 0)),
            pl.BlockSpec((tm, HEAD_DIM), lambda i: (i % nseq, 0)),
        ],
        out_specs=[
            pl.BlockSpec((tm, ATTN_WIDTH), lambda i: (i, 0)),
            pl.BlockSpec((tm, KV_WIDTH), lambda i: (i, 0)),
        ],
        out_shape=[
            jax.ShapeDtypeStruct((t, ATTN_WIDTH), BF16),
            jax.ShapeDtypeStruct((t, KV_WIDTH), BF16),
        ],
        compiler_params=_params(1),
        name="qk_prep",
    )(z2, z2, q_gain, k_gain, cos, sin)


ATTN_TQ = 128
ATTN_TK = 512
ATTN_UNROLL = 8
SUBLANES = 8
NT_DIMS = (((1,), (1,)), ((), ()))


def _attn_kernel(q_ref, k_ref, v_ref, o_ref, sa_ref, sb_ref, vt_ref, acc_ref):
    seq = q_ref.shape[0]
    n_tiles = seq // ATTN_TQ
    n_chunks = seq // ATTN_TK
    assert n_tiles % 2 == 0 and n_tiles >= 2
    mq = GROUP * ATTN_TQ
    fold = (ATTN_TK // SUBLANES, SUBLANES, mq)

    for c in range(n_chunks):
        vt_ref[c] = v_ref[c * ATTN_TK:(c + 1) * ATTN_TK, :].T

    def tile_rows(t):
        return pl.ds(pl.multiple_of(t * ATTN_TQ, ATTN_TQ), ATTN_TQ)

    def chunk_rows(c):
        return pl.ds(pl.multiple_of(c * ATTN_TK, ATTN_TK), ATTN_TK)

    def load_q(t):
        rows = tile_rows(t)
        return jnp.concatenate(
            [q_ref[rows, j * HEAD_DIM:(j + 1) * HEAD_DIM] for j in range(GROUP)], axis=0)

    def col_max(mx8):
        return jnp.broadcast_to(jnp.max(mx8, axis=0, keepdims=True), mx8.shape)

    def qk_chunk(q, s_ref, c, mx8):
        rows = chunk_rows(c)
        s = jax.lax.dot_general(k_ref[rows, :], q, NT_DIMS,
                                preferred_element_type=F32)
        s_ref[rows, :] = s
        return jnp.maximum(mx8, jnp.max(s.reshape(fold), axis=0))

    def pv_chunk(s_ref, c, m8, l8):
        p3 = jnp.exp2(s_ref[chunk_rows(c), :].reshape(fold) - m8[None])
        p = p3.reshape(ATTN_TK, mq).astype(BF16)
        acc_ref[...] += jnp.dot(vt_ref[c], p, preferred_element_type=F32)
        return l8 + jnp.sum(p3, axis=0)

    def finalize(t, l8):
        o_t = acc_ref[...] / jnp.sum(l8, axis=0, keepdims=True)
        rows = tile_rows(t)
        for j in range(GROUP):
            o_ref[rows, j * HEAD_DIM:(j + 1) * HEAD_DIM] = (
                o_t[:, j * ATTN_TQ:(j + 1) * ATTN_TQ].T.astype(BF16))

    neg8 = jnp.full((SUBLANES, mq), -jnp.inf, F32)
    zero8 = jnp.zeros((SUBLANES, mq), F32)

    def scores_only(t, s_ref):
        q = load_q(t)
        return col_max(jax.lax.fori_loop(
            0, n_chunks, lambda c, mx8: qk_chunk(q, s_ref, c, mx8), neg8,
            unroll=ATTN_UNROLL))

    def values_only(t, s_ref, m8):
        acc_ref[...] = jnp.zeros_like(acc_ref)
        l8 = jax.lax.fori_loop(
            0, n_chunks, lambda c, l8: pv_chunk(s_ref, c, m8, l8), zero8,
            unroll=ATTN_UNROLL)
        finalize(t, l8)

    def phase(t, s_write, s_read, m8_prev):
        q = load_q(t)
        acc_ref[...] = jnp.zeros_like(acc_ref)

        def body(c, carry):
            mx8, l8 = carry
            return qk_chunk(q, s_write, c, mx8), pv_chunk(s_read, c, m8_prev, l8)

        mx8, l8 = jax.lax.fori_loop(0, n_chunks, body, (neg8, zero8),
                                    unroll=ATTN_UNROLL)
        finalize(t - 1, l8)
        return col_max(mx8)

    def phase_pair(i, m8):
        t = 2 * i + 1
        m8 = phase(t, sb_ref, sa_ref, m8)
        return phase(t + 1, sa_ref, sb_ref, m8)

    m8 = scores_only(0, sa_ref)
    m8 = jax.lax.fori_loop(0, (n_tiles - 2) // 2, phase_pair, m8)
    m8 = phase(n_tiles - 1, sb_ref, sa_ref, m8)
    values_only(n_tiles - 1, sb_ref, m8)


def _attention(qn, kn, v_src, v_col, batch, seq):
    t = qn.shape[0]
    gw = GROUP * HEAD_DIM
    mq = GROUP * ATTN_TQ
    return pl.pallas_call(
        _attn_kernel,
        grid=(batch, N_KV_HEADS),
        in_specs=[
            pl.BlockSpec((seq, gw), lambda b, g: (b, g)),
            pl.BlockSpec((seq, HEAD_DIM), lambda b, g: (b, g)),
            pl.BlockSpec((seq, HEAD_DIM), lambda b, g: (b, v_col // HEAD_DIM + g)),
        ],
        out_specs=pl.BlockSpec((seq, gw), lambda b, g: (b, g)),
        out_shape=jax.ShapeDtypeStruct((t, ATTN_WIDTH), BF16),
        scratch_shapes=[
            pltpu.VMEM((seq, mq), F32),
            pltpu.VMEM((seq, mq), F32),
            pltpu.VMEM((seq // ATTN_TK, HEAD_DIM, ATTN_TK), BF16),
            pltpu.VMEM((HEAD_DIM, mq), F32),
        ],
        compiler_params=_params(2),
        name="attention",
    )(qn, kn, v_src)


def _gate_kernel(a_ref, o_ref, ga_ref, gb_ref, ba_ref, bb_ref, wc_ref, wa_ref, m_ref):
    ya = jnp.dot(a_ref[...], wc_ref[...], preferred_element_type=F32)
    yb = jnp.dot(o_ref[...], wa_ref[...], preferred_element_type=F32)
    gate_a = jax.nn.sigmoid(ga_ref[...].astype(F32) + ba_ref[...])
    gate_b = jax.nn.sigmoid(gb_ref[...].astype(F32) + bb_ref[...])
    m_ref[...] = (gate_a * ya + gate_b * yb).astype(BF16)


def _gated_branches(a2, o2, z2, bias_a, bias_b, w_conv_out, w_attn_out, layer,
                    tm=1024, tn=1024):
    t = a2.shape[0]
    row_spec = pl.BlockSpec((tm, D_MODEL), lambda i, j: (i, 0))
    zspec = lambda off: pl.BlockSpec((tm, tn), lambda i, j, off=off: (i, off // tn + j))
    bspec = pl.BlockSpec((None, 1, tn), lambda i, j: (layer, 0, j))
    wspec = pl.BlockSpec((None, D_MODEL, tn), lambda i, j: (layer, 0, j))
    return pl.pallas_call(
        _gate_kernel,
        grid=(t // tm, D_MODEL // tn),
        in_specs=[row_spec, row_spec, zspec(Z_GA), zspec(Z_GB), bspec, bspec,
                  wspec, wspec],
        out_specs=pl.BlockSpec((tm, tn), lambda i, j: (i, j)),
        out_shape=jax.ShapeDtypeStruct((t, D_MODEL), BF16),
        compiler_params=_params(2),
        name="gated_branches",
    )(a2, o2, z2, z2, bias_a, bias_b, w_conv_out, w_attn_out)


def _proj_norm_res_kernel(m_ref, w_ref, g_ref, x_ref, o_ref):
    y = jnp.dot(m_ref[...], w_ref[...], preferred_element_type=F32)
    o_ref[...] = x_ref[...] + y * _rms_scale(y) * g_ref[...]


def _proj_norm_res(m2, w, gain, x2, layer, tm=512):
    t = m2.shape[0]
    return pl.pallas_call(
        _proj_norm_res_kernel,
        grid=(t // tm,),
        in_specs=[
            pl.BlockSpec((tm, D_MODEL), lambda i: (i, 0)),
            pl.BlockSpec((None, D_MODEL, D_MODEL), lambda i: (layer, 0, 0)),
            pl.BlockSpec((None, 1, D_MODEL), lambda i: (layer, 0, 0)),
            pl.BlockSpec((tm, D_MODEL), lambda i: (i, 0)),
        ],
        out_specs=pl.BlockSpec((tm, D_MODEL), lambda i: (i, 0)),
        out_shape=jax.ShapeDtypeStruct((t, D_MODEL), F32),
        compiler_params=_params(1),
        name="merge_proj",
    )(m2, w, gain, x2)


def _mlp_kernel(x_ref, gpre_ref, wu_ref, wd_ref, gpost_ref, o_ref, h_ref, acc_ref):
    k = pl.program_id(1)

    @pl.when(k == 0)
    def _():
        xf = x_ref[...]
        h_ref[...] = (xf * _rms_scale(xf) * gpre_ref[...]).astype(BF16)

    up = jnp.dot(h_ref[...], wu_ref[...], preferred_element_type=F32)
    act = jnp.square(jnp.maximum(up, 0.0)).astype(BF16)
    part = jnp.dot(act, wd_ref[...], preferred_element_type=F32)

    @pl.when(k == 0)
    def _():
        acc_ref[...] = part

    @pl.when(k > 0)
    def _():
        acc_ref[...] += part

    @pl.when(k == pl.num_programs(1) - 1)
    def _():
        f = acc_ref[...]
        o_ref[...] = x_ref[...] + f * _rms_scale(f) * gpost_ref[...]


def _mlp(x2, g_pre, w_up, w_down, g_post, layer, tm=512, tf=1024):
    t = x2.shape[0]
    gspec = pl.BlockSpec((None, 1, D_MODEL), lambda i, k: (layer, 0, 0))
    return pl.pallas_call(
        _mlp_kernel,
        grid=(t // tm, D_FF // tf),
        in_specs=[
            pl.BlockSpec((tm, D_MODEL), lambda i, k: (i, 0)),
            gspec,
            pl.BlockSpec((None, D_MODEL, tf), lambda i, k: (layer, 0, k)),
            pl.BlockSpec((None, tf, D_MODEL), lambda i, k: (layer, k, 0)),
            gspec,
        ],
        out_specs=pl.BlockSpec((tm, D_MODEL), lambda i, k: (i, 0)),
        out_shape=jax.ShapeDtypeStruct((t, D_MODEL), F32),
        scratch_shapes=[pltpu.VMEM((tm, D_MODEL), BF16),
                        pltpu.VMEM((tm, D_MODEL), F32)],
        compiler_params=_params(2),
        name="mlp",
    )(x2, g_pre, w_up, w_down, g_post)


def _rope_tables(seq):
    pos = np.arange(seq)
    inv_freq = ROPE_THETA ** (-np.arange(0, AXIS_DIM, 2, dtype=np.float32) / AXIS_DIM)
    inv_freq = jnp.asarray(inv_freq, F32)
    row = jnp.asarray(pos // GRID_W, F32)[:, None] * inv_freq
    col = jnp.asarray(pos % GRID_W, F32)[:, None] * inv_freq
    cos = jnp.concatenate([jnp.cos(row), jnp.cos(row), jnp.cos(col), jnp.cos(col)], axis=1)
    sin = jnp.concatenate([-jnp.sin(row), jnp.sin(row), -jnp.sin(col), jnp.sin(col)], axis=1)
    return cos, sin


def kernel(x, norm_mix_pre, w_in, gate_bias, conv_w, q_norm, k_norm, w_out_conv, w_out_attn,
           w_merge, norm_mix_post, norm_mlp_pre, w_up, w_down, norm_mlp_post):
    b, s, d = x.shape
    depth = w_in.shape[0]
    t = b * s

    o_k = 3 * CONV_WIDTH + ATTN_WIDTH
    o_g = o_k + 2 * KV_WIDTH
    w_in_b = jnp.concatenate(
        [w_in[:, :, :o_k], w_in[:, :, o_g:], w_in[:, :, o_k:o_g]], axis=-1).astype(BF16)
    w_conv_out_b = w_out_conv.astype(BF16)
    w_attn_out_b = w_out_attn.astype(BF16)
    w_merge_b = w_merge.astype(BF16)
    w_up_b = w_up.astype(BF16)
    w_down_b = w_down.astype(BF16)
    row3 = lambda p: p.reshape(depth, 1, p.shape[-1])
    bias_a = row3(gate_bias[:, :D_MODEL])
    bias_b = row3(gate_bias[:, D_MODEL:])
    cos, sin = _rope_tables(s)

    x2 = x.reshape(t, d)
    for l in range(depth):
        z2 = _in_proj(x2, row3(norm_mix_pre), w_in_b, l)
        a3 = _conv_mix(z2.reshape(b, s, IN_WIDTH), conv_w, l)
        qn, kn = _qk_prep(z2, row3(q_norm), row3(k_norm), cos, sin, l, s)
        o2 = _attention(qn, kn, z2, Z_V, b, s)
        m2 = _gated_branches(a3.reshape(t, CONV_WIDTH), o2, z2, bias_a, bias_b,
                             w_conv_out_b, w_attn_out_b, l)
        x2 = _proj_norm_res(m2, w_merge_b, row3(norm_mix_post), x2, l)
        x2 = _mlp(x2, row3(norm_mlp_pre), w_up_b, w_down_b, row3(norm_mlp_post), l)
    return x2.reshape(b, s, d)
```

```python
import functools
import math

import jax
import jax.numpy as jnp
import numpy as np
from jax.experimental import pallas as pl
from jax.experimental.pallas import tpu as pltpu

D_MODEL = 2048
GRID_W = 64
HEAD_DIM = 128
N_Q_HEADS = 16
N_KV_HEADS = 4
GROUP = N_Q_HEADS // N_KV_HEADS
ATTN_WIDTH = N_Q_HEADS * HEAD_DIM
KV_WIDTH = N_KV_HEADS * HEAD_DIM
CONV_WIDTH = D_MODEL
D_FF = 4 * D_MODEL
ROPE_THETA = 10000.0
RMS_EPS = 1e-6
AXIS_DIM = HEAD_DIM // 2
N_FREQ = AXIS_DIM // 2
IN_WIDTH = 3 * CONV_WIDTH + ATTN_WIDTH + 2 * KV_WIDTH + 2 * D_MODEL

Z_B, Z_C, Z_IN, Z_Q, Z_GA, Z_GB = (i * D_MODEL for i in range(6))
Z_K = 6 * D_MODEL
Z_V = Z_K + KV_WIDTH

V7X_VMEM_BYTES = 64 * 1024 * 1024
VMEM_LIMIT = V7X_VMEM_BYTES - 8 * 1024 * 1024

BF16 = jnp.bfloat16
F32 = jnp.float32


def _params(n_axes):
    return pltpu.CompilerParams(
        dimension_semantics=("arbitrary",) * n_axes, vmem_limit_bytes=VMEM_LIMIT)


def _rms_scale(xf):
    return jax.lax.rsqrt(jnp.mean(xf * xf, axis=-1, keepdims=True) + RMS_EPS)


def _in_proj_kernel(x_ref, g_ref, w_ref, z_ref, h_ref):
    @pl.when(pl.program_id(1) == 0)
    def _():
        xf = x_ref[...]
        h_ref[...] = (xf * _rms_scale(xf) * g_ref[...]).astype(BF16)

    z_ref[...] = jnp.dot(h_ref[...], w_ref[...],
                         preferred_element_type=F32).astype(BF16)


def _in_proj(x2, gain, w_in, layer, tm=1024):
    t = x2.shape[0]
    tn = 2 * KV_WIDTH
    kv_block = (3 * CONV_WIDTH + ATTN_WIDTH) // tn
    last_block = IN_WIDTH // tn - 1

    def z_block(j):
        return jnp.where(j < kv_block, j, jnp.where(j == kv_block, last_block, j - 1))

    return pl.pallas_call(
        _in_proj_kernel,
        grid=(t // tm, IN_WIDTH // tn),
        in_specs=[
            pl.BlockSpec((tm, D_MODEL), lambda i, j: (i, 0)),
            pl.BlockSpec((None, 1, D_MODEL), lambda i, j: (layer, 0, 0)),
            pl.BlockSpec((None, D_MODEL, tn), lambda i, j: (layer, 0, j)),
        ],
        out_specs=pl.BlockSpec((tm, tn), lambda i, j: (i, z_block(j))),
        out_shape=jax.ShapeDtypeStruct((t, IN_WIDTH), BF16),
        scratch_shapes=[pltpu.VMEM((tm, D_MODEL), BF16)],
        compiler_params=_params(2),
        name="in_proj",
    )(x2, gain, w_in)


def _conv_kernel(b_ref, c_ref, i_ref, w_ref, o_ref):
    s = b_ref.shape[0]
    u = c_ref[...].astype(F32) * i_ref[...].astype(F32)
    row = jax.lax.broadcasted_iota(jnp.int32, u.shape, 0)
    prev = jnp.where(row == 0, 0.0, pltpu.roll(u, 1, 0))
    nxt = jnp.where(row == s - 1, 0.0, pltpu.roll(u, s - 1, 0))
    w = w_ref[...]
    conv = w[0:1] * prev + w[1:2] * u + w[2:3] * nxt
    o_ref[...] = (b_ref[...].astype(F32) * conv).astype(BF16)


def _conv_mix(z3, conv_w, layer, tc=256):
    b, s, _ = z3.shape
    zspec = lambda off: pl.BlockSpec(
        (None, s, tc), lambda bi, j, off=off: (bi, 0, off // tc + j))
    return pl.pallas_call(
        _conv_kernel,
        grid=(b, CONV_WIDTH // tc),
        in_specs=[
            zspec(Z_B), zspec(Z_C), zspec(Z_IN),
            pl.BlockSpec((None, 3, tc), lambda bi, j: (layer, 0, j)),
        ],
        out_specs=pl.BlockSpec((None, s, tc), lambda bi, j: (bi, 0, j)),
        out_shape=jax.ShapeDtypeStruct((b, s, CONV_WIDTH), BF16),
        compiler_params=_params(2),
        name="conv_mix",
    )(z3, z3, z3, conv_w)


def _rope_head(x, gain, cos, sin_signed):
    lane = jax.lax.broadcasted_iota(jnp.int32, x.shape, 1)
    y = x * _rms_scale(x) * gain
    first_half = (lane % AXIS_DIM) < N_FREQ
    partner = jnp.where(first_half,
                        pltpu.roll(y, HEAD_DIM - N_FREQ, 1),
                        pltpu.roll(y, N_FREQ, 1))
    return y * cos + partner * sin_signed


def _qk_prep_kernel(q_ref, k_ref, qg_ref, kg_ref, cos_ref, sin_ref, qo_ref, ko_ref):
    cos = cos_ref[...]
    sin = sin_ref[...]
    scale = math.log2(math.e) / math.sqrt(HEAD_DIM)
    for h in range(N_Q_HEADS):
        sl = slice(h * HEAD_DIM, (h + 1) * HEAD_DIM)
        y = _rope_head(q_ref[:, sl].astype(F32), qg_ref[...], cos, sin)
        qo_ref[:, sl] = (y * scale).astype(BF16)
    for h in range(N_KV_HEADS):
        sl = slice(h * HEAD_DIM, (h + 1) * HEAD_DIM)
        y = _rope_head(k_ref[:, sl].astype(F32), kg_ref[...], cos, sin)
        ko_ref[:, sl] = y.astype(BF16)


def _qk_prep(z2, q_gain, k_gain, cos, sin, layer, seq, tm=512):
    t = z2.shape[0]
    nseq = seq // tm
    return pl.pallas_call(
        _qk_prep_kernel,
        grid=(t // tm,),
        in_specs=[
            pl.BlockSpec((tm, ATTN_WIDTH), lambda i: (i, Z_Q // ATTN_WIDTH)),
            pl.BlockSpec((tm, KV_WIDTH), lambda i: (i, Z_K // KV_WIDTH)),
            pl.BlockSpec((None, 1, HEAD_DIM), lambda i: (layer, 0, 0)),
            pl.BlockSpec((None, 1, HEAD_DIM), lambda i: (layer, 0, 0)),
            pl.BlockSpec((tm, HEAD_DIM), lambda i: (i % nseq, 0)),
            pl.BlockSpec((tm, HEAD_DIM), lambda i: (i % nseq, 0)),
        ],
        out_specs=[
            pl.BlockSpec((tm, ATTN_WIDTH), lambda i: (i, 0)),
            pl.BlockSpec((tm, KV_WIDTH), lambda i: (i, 0)),
        ],
        out_shape=[
            jax.ShapeDtypeStruct((t, ATTN_WIDTH), BF16),
            jax.ShapeDtypeStruct((t, KV_WIDTH), BF16),
        ],
        compiler_params=_params(1),
        name="qk_prep",
    )(z2, z2, q_gain, k_gain, cos, sin)


ATTN_TQ = 128
ATTN_TK = 512
ATTN_UNROLL = 8
SUBLANES = 8
NT_DIMS = (((1,), (1,)), ((), ()))


def _attn_kernel(q_ref, k_ref, v_ref, o_ref, sa_ref, sb_ref, vt_ref, acc_ref):
    seq = q_ref.shape[0]
    n_tiles = seq // ATTN_TQ
    n_chunks = seq // ATTN_TK
    assert n_tiles % 2 == 0 and n_tiles >= 2
    mq = GROUP * ATTN_TQ
    fold = (ATTN_TK // SUBLANES, SUBLANES, mq)

    for c in range(n_chunks):
        vt_ref[c] = v_ref[c * ATTN_TK:(c + 1) * ATTN_TK, :].T

    def tile_rows(t):
        return pl.ds(pl.multiple_of(t * ATTN_TQ, ATTN_TQ), ATTN_TQ)

    def chunk_rows(c):
        return pl.ds(pl.multiple_of(c * ATTN_TK, ATTN_TK), ATTN_TK)

    def load_q(t):
        rows = tile_rows(t)
        return jnp.concatenate(
            [q_ref[rows, j * HEAD_DIM:(j + 1) * HEAD_DIM] for j in range(GROUP)], axis=0)

    def col_max(mx8):
        return jnp.broadcast_to(jnp.max(mx8, axis=0, keepdims=True), mx8.shape)

    def qk_chunk(q, s_ref, c, mx8):
        rows = chunk_rows(c)
        s = jax.lax.dot_general(k_ref[rows, :], q, NT_DIMS,
                                preferred_element_type=F32)
        s_ref[rows, :] = s
        return jnp.maximum(mx8, jnp.max(s.reshape(fold), axis=0))

    def pv_chunk(s_ref, c, m8, l8):
        p3 = jnp.exp2(s_ref[chunk_rows(c), :].reshape(fold) - m8[None])
        p = p3.reshape(ATTN_TK, mq).astype(BF16)
        acc_ref[...] += jnp.dot(vt_ref[c], p, preferred_element_type=F32)
        return l8 + jnp.sum(p3, axis=0)

    def finalize(t, l8):
        o_t = acc_ref[...] / jnp.sum(l8, axis=0, keepdims=True)
        rows = tile_rows(t)
        for j in range(GROUP):
            o_ref[rows, j * HEAD_DIM:(j + 1) * HEAD_DIM] = (
                o_t[:, j * ATTN_TQ:(j + 1) * ATTN_TQ].T.astype(BF16))

    neg8 = jnp.full((SUBLANES, mq), -jnp.inf, F32)
    zero8 = jnp.zeros((SUBLANES, mq), F32)

    def scores_only(t, s_ref):
        q = load_q(t)
        return col_max(jax.lax.fori_loop(
            0, n_chunks, lambda c, mx8: qk_chunk(q, s_ref, c, mx8), neg8,
            unroll=ATTN_UNROLL))

    def values_only(t, s_ref, m8):
        acc_ref[...] = jnp.zeros_like(acc_ref)
        l8 = jax.lax.fori_loop(
            0, n_chunks, lambda c, l8: pv_chunk(s_ref, c, m8, l8), zero8,
            unroll=ATTN_UNROLL)
        finalize(t, l8)

    def phase(t, s_write, s_read, m8_prev):
        q = load_q(t)
        acc_ref[...] = jnp.zeros_like(acc_ref)

        def body(c, carry):
            mx8, l8 = carry
            return qk_chunk(q, s_write, c, mx8), pv_chunk(s_read, c, m8_prev, l8)

        mx8, l8 = jax.lax.fori_loop(0, n_chunks, body, (neg8, zero8),
                                    unroll=ATTN_UNROLL)
        finalize(t - 1, l8)
        return col_max(mx8)

    def phase_pair(i, m8):
        t = 2 * i + 1
        m8 = phase(t, sb_ref, sa_ref, m8)
        return phase(t + 1, sa_ref, sb_ref, m8)

    m8 = scores_only(0, sa_ref)
    m8 = jax.lax.fori_loop(0, (n_tiles - 2) // 2, phase_pair, m8)
    m8 = phase(n_tiles - 1, sb_ref, sa_ref, m8)
    values_only(n_tiles - 1, sb_ref, m8)


def _attention(qn, kn, v_src, v_col, batch, seq):
    t = qn.shape[0]
    gw = GROUP * HEAD_DIM
    mq = GROUP * ATTN_TQ
    return pl.pallas_call(
        _attn_kernel,
        grid=(batch, N_KV_HEADS),
        in_specs=[
            pl.BlockSpec((seq, gw), lambda b, g: (b, g)),
            pl.BlockSpec((seq, HEAD_DIM), lambda b, g: (b, g)),
            pl.BlockSpec((seq, HEAD_DIM), lambda b, g: (b, v_col // HEAD_DIM + g)),
        ],
        out_specs=pl.BlockSpec((seq, gw), lambda b, g: (b, g)),
        out_shape=jax.ShapeDtypeStruct((t, ATTN_WIDTH), BF16),
        scratch_shapes=[
            pltpu.VMEM((seq, mq), F32),
            pltpu.VMEM((seq, mq), F32),
            pltpu.VMEM((seq // ATTN_TK, HEAD_DIM, ATTN_TK), BF16),
            pltpu.VMEM((HEAD_DIM, mq), F32),
        ],
        compiler_params=_params(2),
        name="attention",
    )(qn, kn, v_src)


def _gate_kernel(a_ref, o_ref, ga_ref, gb_ref, ba_ref, bb_ref, wc_ref, wa_ref, m_ref):
    ya = jnp.dot(a_ref[...], wc_ref[...], preferred_element_type=F32)
    yb = jnp.dot(o_ref[...], wa_ref[...], preferred_element_type=F32)
    gate_a = jax.nn.sigmoid(ga_ref[...].astype(F32) + ba_ref[...])
    gate_b = jax.nn.sigmoid(gb_ref[...].astype(F32) + bb_ref[...])
    m_ref[...] = (gate_a * ya + gate_b * yb).astype(BF16)


def _gated_branches(a2, o2, z2, bias_a, bias_b, w_conv_out, w_attn_out, layer,
                    tm=1024, tn=1024):
    t = a2.shape[0]
    row_spec = pl.BlockSpec((tm, D_MODEL), lambda i, j: (i, 0))
    zspec = lambda off: pl.BlockSpec((tm, tn), lambda i, j, off=off: (i, off // tn + j))
    bspec = pl.BlockSpec((None, 1, tn), lambda i, j: (layer, 0, j))
    wspec = pl.BlockSpec((None, D_MODEL, tn), lambda i, j: (layer, 0, j))
    return pl.pallas_call(
        _gate_kernel,
        grid=(t // tm, D_MODEL // tn),
        in_specs=[row_spec, row_spec, zspec(Z_GA), zspec(Z_GB), bspec, bspec,
                  wspec, wspec],
        out_specs=pl.BlockSpec((tm, tn), lambda i, j: (i, j)),
        out_shape=jax.ShapeDtypeStruct((t, D_MODEL), BF16),
        compiler_params=_params(2),
        name="gated_branches",
    )(a2, o2, z2, z2, bias_a, bias_b, w_conv_out, w_attn_out)


def _proj_norm_res_kernel(m_ref, w_ref, g_ref, x_ref, o_ref):
    y = jnp.dot(m_ref[...], w_ref[...], preferred_element_type=F32)
    o_ref[...] = x_ref[...] + y * _rms_scale(y) * g_ref[...]


def _proj_norm_res(m2, w, gain, x2, layer, tm=512):
    t = m2.shape[0]
    return pl.pallas_call(
        _proj_norm_res_kernel,
        grid=(t // tm,),
        in_specs=[
            pl.BlockSpec((tm, D_MODEL), lambda i: (i, 0)),
            pl.BlockSpec((None, D_MODEL, D_MODEL), lambda i: (layer, 0, 0)),
            pl.BlockSpec((None, 1, D_MODEL), lambda i: (layer, 0, 0)),
            pl.BlockSpec((tm, D_MODEL), lambda i: (i, 0)),
        ],
        out_specs=pl.BlockSpec((tm, D_MODEL), lambda i: (i, 0)),
        out_shape=jax.ShapeDtypeStruct((t, D_MODEL), F32),
        compiler_params=_params(1),
        name="merge_proj",
    )(m2, w, gain, x2)


MLP_TM = 256
MLP_UP_TN = 1024


def _mlp_up_kernel(x_ref, g_ref, w_ref, a_ref):
    xf = x_ref[...]
    h = (xf * _rms_scale(xf) * g_ref[...]).astype(BF16)
    for n in range(D_FF // MLP_UP_TN):
        cols = slice(n * MLP_UP_TN, (n + 1) * MLP_UP_TN)
        up = jnp.dot(h, w_ref[:, cols], preferred_element_type=F32)
        a_ref[:, cols] = jnp.square(jnp.maximum(up, 0.0)).astype(BF16)


def _mlp_down_kernel(a_ref, w_ref, g_ref, x_ref, o_ref):
    f = jnp.dot(a_ref[...], w_ref[...], preferred_element_type=F32)
    o_ref[...] = x_ref[...] + f * _rms_scale(f) * g_ref[...]


def _resident(block_shape, index_map):
    return pl.BlockSpec(block_shape, index_map, pipeline_mode=pl.Buffered(1))


def _mlp(x2, g_pre, w_up, w_down, g_post, layer, tm=MLP_TM):
    t = x2.shape[0]
    gspec = pl.BlockSpec((None, 1, D_MODEL), lambda i: (layer, 0, 0))
    xspec = pl.BlockSpec((tm, D_MODEL), lambda i: (i, 0))
    aspec = pl.BlockSpec((tm, D_FF), lambda i: (i, 0))
    act = pl.pallas_call(
        _mlp_up_kernel,
        grid=(t // tm,),
        in_specs=[xspec, gspec,
                  _resident((None, D_MODEL, D_FF), lambda i: (layer, 0, 0))],
        out_specs=aspec,
        out_shape=jax.ShapeDtypeStruct((t, D_FF), BF16),
        compiler_params=_params(1),
        name="mlp_up",
    )(x2, g_pre, w_up)
    return pl.pallas_call(
        _mlp_down_kernel,
        grid=(t // tm,),
        in_specs=[aspec,
                  _resident((None, D_FF, D_MODEL), lambda i: (layer, 0, 0)),
                  gspec, xspec],
        out_specs=xspec,
        out_shape=jax.ShapeDtypeStruct((t, D_MODEL), F32),
        compiler_params=_params(1),
        name="mlp_down",
    )(act, w_down, g_post, x2)


def _rope_tables(seq):
    pos = np.arange(seq)
    inv_freq = ROPE_THETA ** (-np.arange(0, AXIS_DIM, 2, dtype=np.float32) / AXIS_DIM)
    inv_freq = jnp.asarray(inv_freq, F32)
    row = jnp.asarray(pos // GRID_W, F32)[:, None] * inv_freq
    col = jnp.asarray(pos % GRID_W, F32)[:, None] * inv_freq
    cos = jnp.concatenate([jnp.cos(row), jnp.cos(row), jnp.cos(col), jnp.cos(col)], axis=1)
    sin = jnp.concatenate([-jnp.sin(row), jnp.sin(row), -jnp.sin(col), jnp.sin(col)], axis=1)
    return cos, sin


def kernel(x, norm_mix_pre, w_in, gate_bias, conv_w, q_norm, k_norm, w_out_conv, w_out_attn,
           w_merge, norm_mix_post, norm_mlp_pre, w_up, w_down, norm_mlp_post):
    b, s, d = x.shape
    depth = w_in.shape[0]
    t = b * s

    w_in_b = w_in.astype(BF16)
    w_conv_out_b = w_out_conv.astype(BF16)
    w_attn_out_b = w_out_attn.astype(BF16)
    w_merge_b = w_merge.astype(BF16)
    w_up_b = w_up.astype(BF16)
    w_down_b = w_down.astype(BF16)
    row3 = lambda p: p.reshape(depth, 1, p.shape[-1])
    bias_a = row3(gate_bias[:, :D_MODEL])
    bias_b = row3(gate_bias[:, D_MODEL:])
    cos, sin = _rope_tables(s)

    x2 = x.reshape(t, d)
    for l in range(depth):
        z2 = _in_proj(x2, row3(norm_mix_pre), w_in_b, l)
        a3 = _conv_mix(z2.reshape(b, s, IN_WIDTH), conv_w, l)
        qn, kn = _qk_prep(z2, row3(q_norm), row3(k_norm), cos, sin, l, s)
        o2 = _attention(qn, kn, z2, Z_V, b, s)
        m2 = _gated_branches(a3.reshape(t, CONV_WIDTH), o2, z2, bias_a, bias_b,
                             w_conv_out_b, w_attn_out_b, l)
        x2 = _proj_norm_res(m2, w_merge_b, row3(norm_mix_post), x2, l)
        x2 = _mlp(x2, row3(norm_mlp_pre), w_up_b, w_down_b, row3(norm_mlp_post), l)
    return x2.reshape(b, s, d)
```

```python
import math

import jax
import jax.numpy as jnp
import numpy as np
from jax.experimental import pallas as pl
from jax.experimental.pallas import tpu as pltpu

D_MODEL = 2048
GRID_W = 64
HEAD_DIM = 128
N_Q_HEADS = 16
N_KV_HEADS = 4
GROUP = N_Q_HEADS // N_KV_HEADS
ATTN_WIDTH = N_Q_HEADS * HEAD_DIM
KV_WIDTH = N_KV_HEADS * HEAD_DIM
CONV_WIDTH = D_MODEL
D_FF = 4 * D_MODEL
ROPE_THETA = 10000.0
RMS_EPS = 1e-6
AXIS_DIM = HEAD_DIM // 2
N_FREQ = AXIS_DIM // 2

ZC_WIDTH = 3 * CONV_WIDTH
ZC_B, ZC_C, ZC_IN = 0, CONV_WIDTH, 2 * CONV_WIDTH
ZA_WIDTH = ATTN_WIDTH + 2 * KV_WIDTH + 2 * D_MODEL
ZA_Q = 0
ZA_K = ATTN_WIDTH
ZA_V = ZA_K + KV_WIDTH
ZA_GA = ZA_V + KV_WIDTH
ZA_GB = ZA_GA + D_MODEL

V7X_VMEM_BYTES = 64 * 1024 * 1024
VMEM_LIMIT = V7X_VMEM_BYTES - 8 * 1024 * 1024
SUBLANES = 8

BF16 = jnp.bfloat16
F32 = jnp.float32


def _params(n_axes):
    return pltpu.CompilerParams(
        dimension_semantics=("arbitrary",) * n_axes, vmem_limit_bytes=VMEM_LIMIT)


def _resident(block_shape, index_map):
    return pl.BlockSpec(block_shape, index_map, pipeline_mode=pl.Buffered(1))


def _rms_scale(xf):
    return jax.lax.rsqrt(jnp.mean(xf * xf, axis=-1, keepdims=True) + RMS_EPS)


ROW_TM = 256
DOT_TN = 1024


def _norm_proj_kernel(x_ref, g_ref, w_ref, z_ref):
    xf = x_ref[...]
    h = (xf * _rms_scale(xf) * g_ref[...]).astype(BF16)
    for n in range(w_ref.shape[1] // DOT_TN):
        cols = slice(n * DOT_TN, (n + 1) * DOT_TN)
        z_ref[:, cols] = jnp.dot(h, w_ref[:, cols],
                                 preferred_element_type=F32).astype(BF16)


def _norm_proj(x2, gain, w, layer, name, tm=ROW_TM):
    t = x2.shape[0]
    width = w.shape[-1]
    return pl.pallas_call(
        _norm_proj_kernel,
        grid=(t // tm,),
        in_specs=[
            pl.BlockSpec((tm, D_MODEL), lambda i: (i, 0)),
            pl.BlockSpec((None, 1, D_MODEL), lambda i: (layer, 0, 0)),
            _resident((None, D_MODEL, width), lambda i: (layer, 0, 0)),
        ],
        out_specs=pl.BlockSpec((tm, width), lambda i: (i, 0)),
        out_shape=jax.ShapeDtypeStruct((t, width), BF16),
        compiler_params=_params(1),
        name=name,
    )(x2, gain, w)


def _conv_kernel(b_ref, c_ref, i_ref, w_ref, o_ref):
    s = b_ref.shape[0]
    u = c_ref[...].astype(F32) * i_ref[...].astype(F32)
    row = jax.lax.broadcasted_iota(jnp.int32, u.shape, 0)
    prev = jnp.where(row == 0, 0.0, pltpu.roll(u, 1, 0))
    nxt = jnp.where(row == s - 1, 0.0, pltpu.roll(u, s - 1, 0))
    w = w_ref[...]
    conv = w[0:1] * prev + w[1:2] * u + w[2:3] * nxt
    o_ref[...] = (b_ref[...].astype(F32) * conv).astype(BF16)


def _conv_mix(zc3, conv_w, layer, tc=256):
    b, s, _ = zc3.shape
    zspec = lambda off: pl.BlockSpec(
        (None, s, tc), lambda bi, j, off=off: (bi, 0, off // tc + j))
    return pl.pallas_call(
        _conv_kernel,
        grid=(b, CONV_WIDTH // tc),
        in_specs=[
            zspec(ZC_B), zspec(ZC_C), zspec(ZC_IN),
            pl.BlockSpec((None, 3, tc), lambda bi, j: (layer, 0, j)),
        ],
        out_specs=pl.BlockSpec((None, s, tc), lambda bi, j: (bi, 0, j)),
        out_shape=jax.ShapeDtypeStruct((b, s, CONV_WIDTH), BF16),
        compiler_params=_params(2),
        name="conv_mix",
    )(zc3, zc3, zc3, conv_w)


ATTN_TQ = 128
ATTN_TK = 512
NT_DIMS = (((1,), (1,)), ((), ()))
Q_SCALE = math.log2(math.e) / math.sqrt(HEAD_DIM)


def _rope_head(x, gain, cos, sin_signed):
    lane = jax.lax.broadcasted_iota(jnp.int32, x.shape, 1)
    y = x * _rms_scale(x) * gain
    first_half = (lane % AXIS_DIM) < N_FREQ
    partner = jnp.where(first_half,
                        pltpu.roll(y, HEAD_DIM - N_FREQ, 1),
                        pltpu.roll(y, N_FREQ, 1))
    return y * cos + partner * sin_signed


def _attn_kernel(q_ref, k_ref, v_ref, qg_ref, kg_ref, cos_ref, sin_ref, o_ref,
                 sa_ref, sb_ref, kn_ref, vt_ref, qa_ref, qb_ref, acc_ref):
    seq = q_ref.shape[0]
    n_tiles = seq // ATTN_TQ
    n_chunks = seq // ATTN_TK
    assert n_tiles % 2 == 0 and n_tiles >= 2
    mq = GROUP * ATTN_TQ
    fold = (ATTN_TK // SUBLANES, SUBLANES, mq)

    def tile_rows(t):
        return pl.ds(pl.multiple_of(t * ATTN_TQ, ATTN_TQ), ATTN_TQ)

    def chunk_rows(c):
        return pl.ds(pl.multiple_of(c * ATTN_TK, ATTN_TK), ATTN_TK)

    for c in range(n_chunks):
        vt_ref[c] = v_ref[c * ATTN_TK:(c + 1) * ATTN_TK, :].T

    def prep_k(c, carry):
        rows = chunk_rows(c)
        kn_ref[rows, :] = _rope_head(k_ref[rows, :].astype(F32), kg_ref[...],
                                     cos_ref[rows, :], sin_ref[rows, :]).astype(BF16)
        return carry

    jax.lax.fori_loop(0, n_chunks, prep_k, 0)

    def prep_q(t, dst_ref):
        rows = tile_rows(t)
        cos = cos_ref[rows, :]
        sin = sin_ref[rows, :]
        for j in range(GROUP):
            x = q_ref[rows, j * HEAD_DIM:(j + 1) * HEAD_DIM].astype(F32)
            y = _rope_head(x, qg_ref[...], cos, sin) * Q_SCALE
            dst_ref[j * ATTN_TQ:(j + 1) * ATTN_TQ, :] = y.astype(BF16)

    def col_max(mx8):
        return jnp.broadcast_to(jnp.max(mx8, axis=0, keepdims=True), mx8.shape)

    def qk_chunk(q, s_ref, c, mx8):
        rows = chunk_rows(c)
        s = jax.lax.dot_general(kn_ref[rows, :], q, NT_DIMS,
                                preferred_element_type=F32)
        s_ref[rows, :] = s
        return jnp.maximum(mx8, jnp.max(s.reshape(fold), axis=0))

    def pv_chunk(s_ref, c, m8, l8):
        p3 = jnp.exp2(s_ref[chunk_rows(c), :].reshape(fold) - m8[None])
        p = p3.reshape(ATTN_TK, mq).astype(BF16)
        acc_ref[...] += jnp.dot(vt_ref[c], p, preferred_element_type=F32)
        return l8 + jnp.sum(p3, axis=0)

    def finalize(t, l8):
        o_t = acc_ref[...] / jnp.sum(l8, axis=0, keepdims=True)
        rows = tile_rows(t)
        for j in range(GROUP):
            o_ref[rows, j * HEAD_DIM:(j + 1) * HEAD_DIM] = (
                o_t[:, j * ATTN_TQ:(j + 1) * ATTN_TQ].T.astype(BF16))

    neg8 = jnp.full((SUBLANES, mq), -jnp.inf, F32)
    zero8 = jnp.zeros((SUBLANES, mq), F32)

    def scores_only(q_cur, q_next, s_ref):
        prep_q(1, q_next)
        q = q_cur[...]
        return col_max(jax.lax.fori_loop(
            0, n_chunks, lambda c, mx8: qk_chunk(q, s_ref, c, mx8), neg8,
            unroll=True))

    def values_only(t, s_ref, m8):
        acc_ref[...] = jnp.zeros_like(acc_ref)
        l8 = jax.lax.fori_loop(
            0, n_chunks, lambda c, l8: pv_chunk(s_ref, c, m8, l8), zero8,
            unroll=True)
        finalize(t, l8)

    def phase(t, q_cur, q_next, s_write, s_read, m8_prev):
        if q_next is not None:
            prep_q(t + 1, q_next)
        q = q_cur[...]
        acc_ref[...] = jnp.zeros_like(acc_ref)

        def body(c, carry):
            mx8, l8 = carry
            return qk_chunk(q, s_write, c, mx8), pv_chunk(s_read, c, m8_prev, l8)

        mx8, l8 = jax.lax.fori_loop(0, n_chunks, body, (neg8, zero8), unroll=True)
        finalize(t - 1, l8)
        return col_max(mx8)

    def phase_pair(i, m8):
        t = 2 * i + 1
        m8 = phase(t, qb_ref, qa_ref, sb_ref, sa_ref, m8)
        return phase(t + 1, qa_ref, qb_ref, sa_ref, sb_ref, m8)

    prep_q(0, qa_ref)
    m8 = scores_only(qa_ref, qb_ref, sa_ref)
    m8 = jax.lax.fori_loop(0, (n_tiles - 2) // 2, phase_pair, m8)
    m8 = phase(n_tiles - 1, qb_ref, None, sb_ref, sa_ref, m8)
    values_only(n_tiles - 1, sb_ref, m8)


def _attention(za, q_gain, k_gain, cos, sin, layer, batch, seq):
    t = za.shape[0]
    gw = GROUP * HEAD_DIM
    mq = GROUP * ATTN_TQ
    gain_spec = pl.BlockSpec((None, 1, HEAD_DIM), lambda b, g: (layer, 0, 0))
    table_spec = _resident((seq, HEAD_DIM), lambda b, g: (0, 0))
    return pl.pallas_call(
        _attn_kernel,
        grid=(batch, N_KV_HEADS),
        in_specs=[
            pl.BlockSpec((seq, gw), lambda b, g: (b, ZA_Q // gw + g)),
            pl.BlockSpec((seq, HEAD_DIM), lambda b, g: (b, ZA_K // HEAD_DIM + g)),
            pl.BlockSpec((seq, HEAD_DIM), lambda b, g: (b, ZA_V // HEAD_DIM + g)),
            gain_spec, gain_spec, table_spec, table_spec,
        ],
        out_specs=pl.BlockSpec((seq, gw), lambda b, g: (b, g)),
        out_shape=jax.ShapeDtypeStruct((t, ATTN_WIDTH), BF16),
        scratch_shapes=[
            pltpu.VMEM((seq, mq), F32),
            pltpu.VMEM((seq, mq), F32),
            pltpu.VMEM((seq, HEAD_DIM), BF16),
            pltpu.VMEM((seq // ATTN_TK, HEAD_DIM, ATTN_TK), BF16),
            pltpu.VMEM((mq, HEAD_DIM), BF16),
            pltpu.VMEM((mq, HEAD_DIM), BF16),
            pltpu.VMEM((HEAD_DIM, mq), F32),
        ],
        compiler_params=_params(2),
        name="attention",
    )(za, za, za, q_gain, k_gain, cos, sin)


def _gate_kernel(a_ref, o_ref, ga_ref, gb_ref, ba_ref, bb_ref, wc_ref, wa_ref, m_ref):
    ya = jnp.dot(a_ref[...], wc_ref[...], preferred_element_type=F32)
    yb = jnp.dot(o_ref[...], wa_ref[...], preferred_element_type=F32)
    gate_a = jax.nn.sigmoid(ga_ref[...].astype(F32) + ba_ref[...])
    gate_b = jax.nn.sigmoid(gb_ref[...].astype(F32) + bb_ref[...])
    m_ref[...] = (gate_a * ya + gate_b * yb).astype(BF16)


def _gated_branches(a2, o2, za, bias_a, bias_b, w_conv_out, w_attn_out, layer,
                    tm=1024, tn=1024):
    t = a2.shape[0]
    row_spec = pl.BlockSpec((tm, D_MODEL), lambda i, j: (i, 0))
    zspec = lambda off: pl.BlockSpec((tm, tn), lambda i, j, off=off: (i, off // tn + j))
    bspec = pl.BlockSpec((None, 1, tn), lambda i, j: (layer, 0, j))
    wspec = pl.BlockSpec((None, D_MODEL, tn), lambda i, j: (layer, 0, j))
    return pl.pallas_call(
        _gate_kernel,
        grid=(t // tm, D_MODEL // tn),
        in_specs=[row_spec, row_spec, zspec(ZA_GA), zspec(ZA_GB), bspec, bspec,
                  wspec, wspec],
        out_specs=pl.BlockSpec((tm, tn), lambda i, j: (i, j)),
        out_shape=jax.ShapeDtypeStruct((t, D_MODEL), BF16),
        compiler_params=_params(2),
        name="gated_branches",
    )(a2, o2, za, za, bias_a, bias_b, w_conv_out, w_attn_out)


def _proj_norm_res_kernel(m_ref, w_ref, g_ref, x_ref, o_ref):
    y = jnp.dot(m_ref[...], w_ref[...], preferred_element_type=F32)
    o_ref[...] = x_ref[...] + y * _rms_scale(y) * g_ref[...]


def _proj_norm_res(m2, w, gain, x2, layer, name, tm=ROW_TM):
    t, k = m2.shape
    xspec = pl.BlockSpec((tm, D_MODEL), lambda i: (i, 0))
    return pl.pallas_call(
        _proj_norm_res_kernel,
        grid=(t // tm,),
        in_specs=[
            pl.BlockSpec((tm, k), lambda i: (i, 0)),
            _resident((None, k, D_MODEL), lambda i: (layer, 0, 0)),
            pl.BlockSpec((None, 1, D_MODEL), lambda i: (layer, 0, 0)),
            xspec,
        ],
        out_specs=xspec,
        out_shape=jax.ShapeDtypeStruct((t, D_MODEL), F32),
        compiler_params=_params(1),
        name=name,
    )(m2, w, gain, x2)


def _mlp_up_kernel(x_ref, g_ref, w_ref, a_ref):
    xf = x_ref[...]
    h = (xf * _rms_scale(xf) * g_ref[...]).astype(BF16)
    for n in range(D_FF // DOT_TN):
        cols = slice(n * DOT_TN, (n + 1) * DOT_TN)
        up = jnp.dot(h, w_ref[:, cols], preferred_element_type=F32)
        a_ref[:, cols] = jnp.square(jnp.maximum(up, 0.0)).astype(BF16)


def _mlp_up(x2, g_pre, w_up, layer, tm=ROW_TM):
    t = x2.shape[0]
    return pl.pallas_call(
        _mlp_up_kernel,
        grid=(t // tm,),
        in_specs=[
            pl.BlockSpec((tm, D_MODEL), lambda i: (i, 0)),
            pl.BlockSpec((None, 1, D_MODEL), lambda i: (layer, 0, 0)),
            _resident((None, D_MODEL, D_FF), lambda i: (layer, 0, 0)),
        ],
        out_specs=pl.BlockSpec((tm, D_FF), lambda i: (i, 0)),
        out_shape=jax.ShapeDtypeStruct((t, D_FF), BF16),
        compiler_params=_params(1),
        name="mlp_up",
    )(x2, g_pre, w_up)


def _rope_tables(seq):
    pos = np.arange(seq)
    inv_freq = ROPE_THETA ** (-np.arange(0, AXIS_DIM, 2, dtype=np.float32) / AXIS_DIM)
    inv_freq = jnp.asarray(inv_freq, F32)
    row = jnp.asarray(pos // GRID_W, F32)[:, None] * inv_freq
    col = jnp.asarray(pos % GRID_W, F32)[:, None] * inv_freq
    cos = jnp.concatenate([jnp.cos(row), jnp.cos(row), jnp.cos(col), jnp.cos(col)], axis=1)
    sin = jnp.concatenate([-jnp.sin(row), jnp.sin(row), -jnp.sin(col), jnp.sin(col)], axis=1)
    return cos, sin


def kernel(x, norm_mix_pre, w_in, gate_bias, conv_w, q_norm, k_norm, w_out_conv, w_out_attn,
           w_merge, norm_mix_post, norm_mlp_pre, w_up, w_down, norm_mlp_post):
    b, s, d = x.shape
    depth = w_in.shape[0]
    t = b * s

    w_in_c = w_in[:, :, :ZC_WIDTH].astype(BF16)
    w_in_a = w_in[:, :, ZC_WIDTH:].astype(BF16)
    w_conv_out_b = w_out_conv.astype(BF16)
    w_attn_out_b = w_out_attn.astype(BF16)
    w_merge_b = w_merge.astype(BF16)
    w_up_b = w_up.astype(BF16)
    w_down_b = w_down.astype(BF16)
    row3 = lambda p: p.reshape(depth, 1, p.shape[-1])
    bias_a = row3(gate_bias[:, :D_MODEL])
    bias_b = row3(gate_bias[:, D_MODEL:])
    cos, sin = _rope_tables(s)

    x2 = x.reshape(t, d)
    for l in range(depth):
        zc = _norm_proj(x2, row3(norm_mix_pre), w_in_c, l, "in_proj_conv")
        za = _norm_proj(x2, row3(norm_mix_pre), w_in_a, l, "in_proj_attn")
        a3 = _conv_mix(zc.reshape(b, s, ZC_WIDTH), conv_w, l)
        o2 = _attention(za, row3(q_norm), row3(k_norm), cos, sin, l, b, s)
        m2 = _gated_branches(a3.reshape(t, CONV_WIDTH), o2, za, bias_a, bias_b,
                             w_conv_out_b, w_attn_out_b, l)
        x2 = _proj_norm_res(m2, w_merge_b, row3(norm_mix_post), x2, l, "merge_proj")
        act = _mlp_up(x2, row3(norm_mlp_pre), w_up_b, l)
        x2 = _proj_norm_res(act, w_down_b, row3(norm_mlp_post), x2, l, "mlp_down")
    return x2.reshape(b, s, d)
```

```python
import math

import jax
import jax.numpy as jnp
import numpy as np
from jax.experimental import pallas as pl
from jax.experimental.pallas import tpu as pltpu

D_MODEL = 2048
GRID_W = 64
HEAD_DIM = 128
N_Q_HEADS = 16
N_KV_HEADS = 4
GROUP = N_Q_HEADS // N_KV_HEADS
ATTN_WIDTH = N_Q_HEADS * HEAD_DIM
KV_WIDTH = N_KV_HEADS * HEAD_DIM
CONV_WIDTH = D_MODEL
D_FF = 4 * D_MODEL
ROPE_THETA = 10000.0
RMS_EPS = 1e-6
AXIS_DIM = HEAD_DIM // 2
N_FREQ = AXIS_DIM // 2

IN_WIDTH = 3 * CONV_WIDTH + ATTN_WIDTH + 2 * KV_WIDTH + 2 * D_MODEL
Z_B, Z_C, Z_IN = 0, CONV_WIDTH, 2 * CONV_WIDTH
Z_Q = 3 * CONV_WIDTH
Z_K = Z_Q + ATTN_WIDTH
Z_V = Z_K + KV_WIDTH
Z_GA = Z_V + KV_WIDTH
Z_GB = Z_GA + D_MODEL

V7X_VMEM_BYTES = 64 * 1024 * 1024
VMEM_LIMIT = V7X_VMEM_BYTES - 8 * 1024 * 1024
SUBLANES = 8

BF16 = jnp.bfloat16
F32 = jnp.float32


def _params(n_axes):
    return pltpu.CompilerParams(
        dimension_semantics=("arbitrary",) * n_axes, vmem_limit_bytes=VMEM_LIMIT)


def _resident(block_shape, index_map):
    return pl.BlockSpec(block_shape, index_map, pipeline_mode=pl.Buffered(1))


def _rms_scale(xf):
    return jax.lax.rsqrt(jnp.mean(xf * xf, axis=-1, keepdims=True) + RMS_EPS)


ROW_TM = 256
DOT_TN = 1024
IN_PARTS = 2
IN_PART_WIDTH = IN_WIDTH // IN_PARTS
IN_DOT_TN = IN_PART_WIDTH // 2


def _in_proj_kernel(x_ref, g_ref, w_ref, z_ref):
    xf = x_ref[...]
    h = (xf * _rms_scale(xf) * g_ref[...]).astype(BF16)
    for n in range(IN_PART_WIDTH // IN_DOT_TN):
        cols = slice(n * IN_DOT_TN, (n + 1) * IN_DOT_TN)
        z_ref[:, cols] = jnp.dot(h, w_ref[:, cols],
                                 preferred_element_type=F32).astype(BF16)


def _in_proj(x2, gain, w_in, layer, tm=ROW_TM):
    t = x2.shape[0]
    return pl.pallas_call(
        _in_proj_kernel,
        grid=(IN_PARTS, t // tm),
        in_specs=[
            pl.BlockSpec((tm, D_MODEL), lambda p, i: (i, 0)),
            pl.BlockSpec((None, 1, D_MODEL), lambda p, i: (layer, 0, 0)),
            _resident((None, D_MODEL, IN_PART_WIDTH), lambda p, i: (layer, 0, p)),
        ],
        out_specs=pl.BlockSpec((tm, IN_PART_WIDTH), lambda p, i: (i, p)),
        out_shape=jax.ShapeDtypeStruct((t, IN_WIDTH), BF16),
        compiler_params=_params(2),
        name="in_proj",
    )(x2, gain, w_in)


def _conv_kernel(b_ref, c_ref, i_ref, w_ref, o_ref):
    s = b_ref.shape[0]
    u = c_ref[...].astype(F32) * i_ref[...].astype(F32)
    row = jax.lax.broadcasted_iota(jnp.int32, u.shape, 0)
    prev = jnp.where(row == 0, 0.0, pltpu.roll(u, 1, 0))
    nxt = jnp.where(row == s - 1, 0.0, pltpu.roll(u, s - 1, 0))
    w = w_ref[...]
    conv = w[0:1] * prev + w[1:2] * u + w[2:3] * nxt
    o_ref[...] = (b_ref[...].astype(F32) * conv).astype(BF16)


def _conv_mix(z3, conv_w, layer, tc=256):
    b, s, _ = z3.shape
    zspec = lambda off: pl.BlockSpec(
        (None, s, tc), lambda bi, j, off=off: (bi, 0, off // tc + j))
    return pl.pallas_call(
        _conv_kernel,
        grid=(b, CONV_WIDTH // tc),
        in_specs=[
            zspec(Z_B), zspec(Z_C), zspec(Z_IN),
            pl.BlockSpec((None, 3, tc), lambda bi, j: (layer, 0, j)),
        ],
        out_specs=pl.BlockSpec((None, s, tc), lambda bi, j: (bi, 0, j)),
        out_shape=jax.ShapeDtypeStruct((b, s, CONV_WIDTH), BF16),
        compiler_params=_params(2),
        name="conv_mix",
    )(z3, z3, z3, conv_w)


ATTN_TQ = 128
ATTN_TK = 512
NT_DIMS = (((1,), (1,)), ((), ()))
Q_SCALE = math.log2(math.e) / math.sqrt(HEAD_DIM)


def _rope_head(x, gain, cos, sin_signed):
    lane = jax.lax.broadcasted_iota(jnp.int32, x.shape, 1)
    y = x * _rms_scale(x) * gain
    first_half = (lane % AXIS_DIM) < N_FREQ
    partner = jnp.where(first_half,
                        pltpu.roll(y, HEAD_DIM - N_FREQ, 1),
                        pltpu.roll(y, N_FREQ, 1))
    return y * cos + partner * sin_signed


def _attn_kernel(q_ref, k_ref, v_ref, qg_ref, kg_ref, cos_ref, sin_ref, o_ref,
                 sa_ref, sb_ref, kn_ref, vt_ref, qa_ref, qb_ref, acc_ref):
    seq = q_ref.shape[0]
    n_tiles = seq // ATTN_TQ
    n_chunks = seq // ATTN_TK
    assert n_tiles % 2 == 0 and n_tiles >= 2
    mq = GROUP * ATTN_TQ
    fold = (ATTN_TK // SUBLANES, SUBLANES, mq)

    def tile_rows(t):
        return pl.ds(pl.multiple_of(t * ATTN_TQ, ATTN_TQ), ATTN_TQ)

    def chunk_rows(c):
        return pl.ds(pl.multiple_of(c * ATTN_TK, ATTN_TK), ATTN_TK)

    for c in range(n_chunks):
        vt_ref[c] = v_ref[c * ATTN_TK:(c + 1) * ATTN_TK, :].T

    def prep_k(c, carry):
        rows = chunk_rows(c)
        kn_ref[rows, :] = _rope_head(k_ref[rows, :].astype(F32), kg_ref[...],
                                     cos_ref[rows, :], sin_ref[rows, :]).astype(BF16)
        return carry

    jax.lax.fori_loop(0, n_chunks, prep_k, 0)

    def prep_q(t, dst_ref):
        rows = tile_rows(t)
        cos = cos_ref[rows, :]
        sin = sin_ref[rows, :]
        for j in range(GROUP):
            x = q_ref[rows, j * HEAD_DIM:(j + 1) * HEAD_DIM].astype(F32)
            y = _rope_head(x, qg_ref[...], cos, sin) * Q_SCALE
            dst_ref[j * ATTN_TQ:(j + 1) * ATTN_TQ, :] = y.astype(BF16)

    def col_max(mx8):
        return jnp.broadcast_to(jnp.max(mx8, axis=0, keepdims=True), mx8.shape)

    def qk_chunk(q, s_ref, c, mx8):
        rows = chunk_rows(c)
        s = jax.lax.dot_general(kn_ref[rows, :], q, NT_DIMS,
                                preferred_element_type=F32)
        s_ref[rows, :] = s
        return jnp.maximum(mx8, jnp.max(s.reshape(fold), axis=0))

    def pv_chunk(s_ref, c, m8, l8):
        p3 = jnp.exp2(s_ref[chunk_rows(c), :].reshape(fold) - m8[None])
        p = p3.reshape(ATTN_TK, mq).astype(BF16)
        acc_ref[...] += jnp.dot(vt_ref[c], p, preferred_element_type=F32)
        return l8 + jnp.sum(p3, axis=0)

    def finalize(t, l8):
        o_t = acc_ref[...] / jnp.sum(l8, axis=0, keepdims=True)
        rows = tile_rows(t)
        for j in range(GROUP):
            o_ref[rows, j * HEAD_DIM:(j + 1) * HEAD_DIM] = (
                o_t[:, j * ATTN_TQ:(j + 1) * ATTN_TQ].T.astype(BF16))

    neg8 = jnp.full((SUBLANES, mq), -jnp.inf, F32)
    zero8 = jnp.zeros((SUBLANES, mq), F32)

    def scores_only(q_cur, q_next, s_ref):
        prep_q(1, q_next)
        q = q_cur[...]
        return col_max(jax.lax.fori_loop(
            0, n_chunks, lambda c, mx8: qk_chunk(q, s_ref, c, mx8), neg8,
            unroll=True))

    def values_only(t, s_ref, m8):
        acc_ref[...] = jnp.zeros_like(acc_ref)
        l8 = jax.lax.fori_loop(
            0, n_chunks, lambda c, l8: pv_chunk(s_ref, c, m8, l8), zero8,
            unroll=True)
        finalize(t, l8)

    def phase(t, q_cur, q_next, s_write, s_read, m8_prev):
        if q_next is not None:
            prep_q(t + 1, q_next)
        q = q_cur[...]
        acc_ref[...] = jnp.zeros_like(acc_ref)

        def body(c, carry):
            mx8, l8 = carry
            return qk_chunk(q, s_write, c, mx8), pv_chunk(s_read, c, m8_prev, l8)

        mx8, l8 = jax.lax.fori_loop(0, n_chunks, body, (neg8, zero8), unroll=True)
        finalize(t - 1, l8)
        return col_max(mx8)

    def phase_pair(i, m8):
        t = 2 * i + 1
        m8 = phase(t, qb_ref, qa_ref, sb_ref, sa_ref, m8)
        return phase(t + 1, qa_ref, qb_ref, sa_ref, sb_ref, m8)

    prep_q(0, qa_ref)
    m8 = scores_only(qa_ref, qb_ref, sa_ref)
    m8 = jax.lax.fori_loop(0, (n_tiles - 2) // 2, phase_pair, m8)
    m8 = phase(n_tiles - 1, qb_ref, None, sb_ref, sa_ref, m8)
    values_only(n_tiles - 1, sb_ref, m8)


def _attention(z2, q_gain, k_gain, cos, sin, layer, batch, seq):
    t = z2.shape[0]
    gw = GROUP * HEAD_DIM
    mq = GROUP * ATTN_TQ
    gain_spec = pl.BlockSpec((None, 1, HEAD_DIM), lambda b, g: (layer, 0, 0))
    table_spec = _resident((seq, HEAD_DIM), lambda b, g: (0, 0))
    return pl.pallas_call(
        _attn_kernel,
        grid=(batch, N_KV_HEADS),
        in_specs=[
            pl.BlockSpec((seq, gw), lambda b, g: (b, Z_Q // gw + g)),
            pl.BlockSpec((seq, HEAD_DIM), lambda b, g: (b, Z_K // HEAD_DIM + g)),
            pl.BlockSpec((seq, HEAD_DIM), lambda b, g: (b, Z_V // HEAD_DIM + g)),
            gain_spec, gain_spec, table_spec, table_spec,
        ],
        out_specs=pl.BlockSpec((seq, gw), lambda b, g: (b, g)),
        out_shape=jax.ShapeDtypeStruct((t, ATTN_WIDTH), BF16),
        scratch_shapes=[
            pltpu.VMEM((seq, mq), F32),
            pltpu.VMEM((seq, mq), F32),
            pltpu.VMEM((seq, HEAD_DIM), BF16),
            pltpu.VMEM((seq // ATTN_TK, HEAD_DIM, ATTN_TK), BF16),
            pltpu.VMEM((mq, HEAD_DIM), BF16),
            pltpu.VMEM((mq, HEAD_DIM), BF16),
            pltpu.VMEM((HEAD_DIM, mq), F32),
        ],
        compiler_params=_params(2),
        name="attention",
    )(z2, z2, z2, q_gain, k_gain, cos, sin)


def _gate_kernel(a_ref, o_ref, ga_ref, gb_ref, ba_ref, bb_ref, wc_ref, wa_ref, m_ref):
    ya = jnp.dot(a_ref[...], wc_ref[...], preferred_element_type=F32)
    yb = jnp.dot(o_ref[...], wa_ref[...], preferred_element_type=F32)
    gate_a = jax.nn.sigmoid(ga_ref[...].astype(F32) + ba_ref[...])
    gate_b = jax.nn.sigmoid(gb_ref[...].astype(F32) + bb_ref[...])
    m_ref[...] = (gate_a * ya + gate_b * yb).astype(BF16)


def _gated_branches(a2, o2, z2, bias_a, bias_b, w_conv_out, w_attn_out, layer,
                    tm=1024, tn=1024):
    t = a2.shape[0]
    row_spec = pl.BlockSpec((tm, D_MODEL), lambda i, j: (i, 0))
    zspec = lambda off: pl.BlockSpec((tm, tn), lambda i, j, off=off: (i, off // tn + j))
    bspec = pl.BlockSpec((None, 1, tn), lambda i, j: (layer, 0, j))
    wspec = pl.BlockSpec((None, D_MODEL, tn), lambda i, j: (layer, 0, j))
    return pl.pallas_call(
        _gate_kernel,
        grid=(t // tm, D_MODEL // tn),
        in_specs=[row_spec, row_spec, zspec(Z_GA), zspec(Z_GB), bspec, bspec,
                  wspec, wspec],
        out_specs=pl.BlockSpec((tm, tn), lambda i, j: (i, j)),
        out_shape=jax.ShapeDtypeStruct((t, D_MODEL), BF16),
        compiler_params=_params(2),
        name="gated_branches",
    )(a2, o2, z2, z2, bias_a, bias_b, w_conv_out, w_attn_out)


MERGE_TM = 512


def _proj_norm_res_kernel(m_ref, w_ref, g_ref, x_ref, o_ref):
    y = jnp.dot(m_ref[...], w_ref[...], preferred_element_type=F32)
    o_ref[...] = x_ref[...] + y * _rms_scale(y) * g_ref[...]


def _proj_norm_res(m2, w, gain, x2, layer, name, tm=ROW_TM):
    t, k = m2.shape
    xspec = pl.BlockSpec((tm, D_MODEL), lambda i: (i, 0))
    return pl.pallas_call(
        _proj_norm_res_kernel,
        grid=(t // tm,),
        in_specs=[
            pl.BlockSpec((tm, k), lambda i: (i, 0)),
            _resident((None, k, D_MODEL), lambda i: (layer, 0, 0)),
            pl.BlockSpec((None, 1, D_MODEL), lambda i: (layer, 0, 0)),
            xspec,
        ],
        out_specs=xspec,
        out_shape=jax.ShapeDtypeStruct((t, D_MODEL), F32),
        compiler_params=_params(1),
        name=name,
    )(m2, w, gain, x2)


def _mlp_up_kernel(x_ref, g_ref, w_ref, a_ref):
    xf = x_ref[...]
    h = (xf * _rms_scale(xf) * g_ref[...]).astype(BF16)
    for n in range(D_FF // DOT_TN):
        cols = slice(n * DOT_TN, (n + 1) * DOT_TN)
        up = jnp.dot(h, w_ref[:, cols], preferred_element_type=F32)
        a_ref[:, cols] = jnp.square(jnp.maximum(up, 0.0)).astype(BF16)


def _mlp_up(x2, g_pre, w_up, layer, tm=ROW_TM):
    t = x2.shape[0]
    return pl.pallas_call(
        _mlp_up_kernel,
        grid=(t // tm,),
        in_specs=[
            pl.BlockSpec((tm, D_MODEL), lambda i: (i, 0)),
            pl.BlockSpec((None, 1, D_MODEL), lambda i: (layer, 0, 0)),
            _resident((None, D_MODEL, D_FF), lambda i: (layer, 0, 0)),
        ],
        out_specs=pl.BlockSpec((tm, D_FF), lambda i: (i, 0)),
        out_shape=jax.ShapeDtypeStruct((t, D_FF), BF16),
        compiler_params=_params(1),
        name="mlp_up",
    )(x2, g_pre, w_up)


def _rope_tables(seq):
    pos = np.arange(seq)
    inv_freq = ROPE_THETA ** (-np.arange(0, AXIS_DIM, 2, dtype=np.float32) / AXIS_DIM)
    inv_freq = jnp.asarray(inv_freq, F32)
    row = jnp.asarray(pos // GRID_W, F32)[:, None] * inv_freq
    col = jnp.asarray(pos % GRID_W, F32)[:, None] * inv_freq
    cos = jnp.concatenate([jnp.cos(row), jnp.cos(row), jnp.cos(col), jnp.cos(col)], axis=1)
    sin = jnp.concatenate([-jnp.sin(row), jnp.sin(row), -jnp.sin(col), jnp.sin(col)], axis=1)
    return cos, sin


def kernel(x, norm_mix_pre, w_in, gate_bias, conv_w, q_norm, k_norm, w_out_conv, w_out_attn,
           w_merge, norm_mix_post, norm_mlp_pre, w_up, w_down, norm_mlp_post):
    b, s, d = x.shape
    depth = w_in.shape[0]
    t = b * s

    w_in_b = w_in.astype(BF16)
    w_conv_out_b = w_out_conv.astype(BF16)
    w_attn_out_b = w_out_attn.astype(BF16)
    w_merge_b = w_merge.astype(BF16)
    w_up_b = w_up.astype(BF16)
    w_down_b = w_down.astype(BF16)
    row3 = lambda p: p.reshape(depth, 1, p.shape[-1])
    bias_a = row3(gate_bias[:, :D_MODEL])
    bias_b = row3(gate_bias[:, D_MODEL:])
    cos, sin = _rope_tables(s)

    x2 = x.reshape(t, d)
    for l in range(depth):
        z2 = _in_proj(x2, row3(norm_mix_pre), w_in_b, l)
        a3 = _conv_mix(z2.reshape(b, s, IN_WIDTH), conv_w, l)
        o2 = _attention(z2, row3(q_norm), row3(k_norm), cos, sin, l, b, s)
        m2 = _gated_branches(a3.reshape(t, CONV_WIDTH), o2, z2, bias_a, bias_b,
                             w_conv_out_b, w_attn_out_b, l)
        x2 = _proj_norm_res(m2, w_merge_b, row3(norm_mix_post), x2, l, "merge_proj",
                            tm=MERGE_TM)
        act = _mlp_up(x2, row3(norm_mlp_pre), w_up_b, l)
        x2 = _proj_norm_res(act, w_down_b, row3(norm_mlp_post), x2, l, "mlp_down")
    return x2.reshape(b, s, d)
```

```python
import math

import jax
import jax.numpy as jnp
import numpy as np
from jax.experimental import pallas as pl
from jax.experimental.pallas import tpu as pltpu

D_MODEL = 2048
GRID_W = 64
HEAD_DIM = 128
N_Q_HEADS = 16
N_KV_HEADS = 4
GROUP = N_Q_HEADS // N_KV_HEADS
ATTN_WIDTH = N_Q_HEADS * HEAD_DIM
KV_WIDTH = N_KV_HEADS * HEAD_DIM
CONV_WIDTH = D_MODEL
D_FF = 4 * D_MODEL
ROPE_THETA = 10000.0
RMS_EPS = 1e-6
AXIS_DIM = HEAD_DIM // 2
N_FREQ = AXIS_DIM // 2

IN_WIDTH = 3 * CONV_WIDTH + ATTN_WIDTH + 2 * KV_WIDTH + 2 * D_MODEL
Z_B, Z_C, Z_IN = 0, CONV_WIDTH, 2 * CONV_WIDTH
Z_Q = 3 * CONV_WIDTH
Z_K = Z_Q + ATTN_WIDTH
Z_V = Z_K + KV_WIDTH
Z_GA = Z_V + KV_WIDTH
Z_GB = Z_GA + D_MODEL

V7X_VMEM_BYTES = 64 * 1024 * 1024
VMEM_LIMIT = V7X_VMEM_BYTES - 8 * 1024 * 1024
SUBLANES = 8

BF16 = jnp.bfloat16
F32 = jnp.float32


def _params(n_axes):
    return pltpu.CompilerParams(
        dimension_semantics=("arbitrary",) * n_axes, vmem_limit_bytes=VMEM_LIMIT)


def _resident(block_shape, index_map):
    return pl.BlockSpec(block_shape, index_map, pipeline_mode=pl.Buffered(1))


def _rms_scale(xf):
    return jax.lax.rsqrt(jnp.mean(xf * xf, axis=-1, keepdims=True) + RMS_EPS)


ROW_TM = 256
DOT_TN = 1024
IN_PARTS = 2
IN_PART_WIDTH = IN_WIDTH // IN_PARTS
IN_DOT_TN = IN_PART_WIDTH // 2


def _in_proj_kernel(x_ref, g_ref, w_ref, wu_ref, wd_ref, z_ref, wub_ref, wdb_ref):
    xf = x_ref[...]
    h = (xf * _rms_scale(xf) * g_ref[...]).astype(BF16)
    for n in range(IN_PART_WIDTH // IN_DOT_TN):
        cols = slice(n * IN_DOT_TN, (n + 1) * IN_DOT_TN)
        z_ref[:, cols] = jnp.dot(h, w_ref[:, cols],
                                 preferred_element_type=F32).astype(BF16)
    wub_ref[...] = wu_ref[...].astype(BF16)
    wdb_ref[...] = wd_ref[...].astype(BF16)


def _in_proj(x2, gain, w_in_b, w_up, w_down, layer, tm=ROW_TM):
    t = x2.shape[0]
    n_rows = t // tm
    steps = IN_PARTS * n_rows
    up_rows = D_MODEL // steps
    down_rows = D_FF // steps
    slab = lambda p, i: p * n_rows + i
    return pl.pallas_call(
        _in_proj_kernel,
        grid=(IN_PARTS, n_rows),
        in_specs=[
            pl.BlockSpec((tm, D_MODEL), lambda p, i: (i, 0)),
            pl.BlockSpec((None, 1, D_MODEL), lambda p, i: (layer, 0, 0)),
            _resident((D_MODEL, IN_PART_WIDTH), lambda p, i: (0, p)),
            pl.BlockSpec((None, up_rows, D_FF), lambda p, i: (layer, slab(p, i), 0)),
            pl.BlockSpec((None, down_rows, D_MODEL), lambda p, i: (layer, slab(p, i), 0)),
        ],
        out_specs=[
            pl.BlockSpec((tm, IN_PART_WIDTH), lambda p, i: (i, p)),
            pl.BlockSpec((up_rows, D_FF), lambda p, i: (slab(p, i), 0)),
            pl.BlockSpec((down_rows, D_MODEL), lambda p, i: (slab(p, i), 0)),
        ],
        out_shape=[
            jax.ShapeDtypeStruct((t, IN_WIDTH), BF16),
            jax.ShapeDtypeStruct((D_MODEL, D_FF), BF16),
            jax.ShapeDtypeStruct((D_FF, D_MODEL), BF16),
        ],
        compiler_params=_params(2),
        name="in_proj",
    )(x2, gain, w_in_b, w_up, w_down)


def _conv_kernel(b_ref, c_ref, i_ref, w_ref, o_ref):
    s = b_ref.shape[0]
    u = c_ref[...].astype(F32) * i_ref[...].astype(F32)
    row = jax.lax.broadcasted_iota(jnp.int32, u.shape, 0)
    prev = jnp.where(row == 0, 0.0, pltpu.roll(u, 1, 0))
    nxt = jnp.where(row == s - 1, 0.0, pltpu.roll(u, s - 1, 0))
    w = w_ref[...]
    conv = w[0:1] * prev + w[1:2] * u + w[2:3] * nxt
    o_ref[...] = (b_ref[...].astype(F32) * conv).astype(BF16)


def _conv_mix(z3, conv_w, layer, tc=256):
    b, s, _ = z3.shape
    zspec = lambda off: pl.BlockSpec(
        (None, s, tc), lambda bi, j, off=off: (bi, 0, off // tc + j))
    return pl.pallas_call(
        _conv_kernel,
        grid=(b, CONV_WIDTH // tc),
        in_specs=[
            zspec(Z_B), zspec(Z_C), zspec(Z_IN),
            pl.BlockSpec((None, 3, tc), lambda bi, j: (layer, 0, j)),
        ],
        out_specs=pl.BlockSpec((None, s, tc), lambda bi, j: (bi, 0, j)),
        out_shape=jax.ShapeDtypeStruct((b, s, CONV_WIDTH), BF16),
        compiler_params=_params(2),
        name="conv_mix",
    )(z3, z3, z3, conv_w)


ATTN_TQ = 128
ATTN_TK = 512
NT_DIMS = (((1,), (1,)), ((), ()))
Q_SCALE = math.log2(math.e) / math.sqrt(HEAD_DIM)


def _rope_head(x, gain, cos, sin_signed):
    lane = jax.lax.broadcasted_iota(jnp.int32, x.shape, 1)
    y = x * _rms_scale(x) * gain
    first_half = (lane % AXIS_DIM) < N_FREQ
    partner = jnp.where(first_half,
                        pltpu.roll(y, HEAD_DIM - N_FREQ, 1),
                        pltpu.roll(y, N_FREQ, 1))
    return y * cos + partner * sin_signed


def _attn_kernel(q_ref, k_ref, v_ref, qg_ref, kg_ref, cos_ref, sin_ref, o_ref,
                 sa_ref, sb_ref, kn_ref, vt_ref, qa_ref, qb_ref, acc_ref):
    seq = q_ref.shape[0]
    n_tiles = seq // ATTN_TQ
    n_chunks = seq // ATTN_TK
    assert n_tiles % 2 == 0 and n_tiles >= 2
    mq = GROUP * ATTN_TQ
    fold = (ATTN_TK // SUBLANES, SUBLANES, mq)

    def tile_rows(t):
        return pl.ds(pl.multiple_of(t * ATTN_TQ, ATTN_TQ), ATTN_TQ)

    def chunk_rows(c):
        return pl.ds(pl.multiple_of(c * ATTN_TK, ATTN_TK), ATTN_TK)

    for c in range(n_chunks):
        vt_ref[c] = v_ref[c * ATTN_TK:(c + 1) * ATTN_TK, :].T

    def prep_k(c, carry):
        rows = chunk_rows(c)
        kn_ref[rows, :] = _rope_head(k_ref[rows, :].astype(F32), kg_ref[...],
                                     cos_ref[rows, :], sin_ref[rows, :]).astype(BF16)
        return carry

    jax.lax.fori_loop(0, n_chunks, prep_k, 0)

    def prep_q(t, dst_ref):
        rows = tile_rows(t)
        cos = cos_ref[rows, :]
        sin = sin_ref[rows, :]
        for j in range(GROUP):
            x = q_ref[rows, j * HEAD_DIM:(j + 1) * HEAD_DIM].astype(F32)
            y = _rope_head(x, qg_ref[...], cos, sin) * Q_SCALE
            dst_ref[j * ATTN_TQ:(j + 1) * ATTN_TQ, :] = y.astype(BF16)

    def col_max(mx8):
        return jnp.broadcast_to(jnp.max(mx8, axis=0, keepdims=True), mx8.shape)

    def qk_chunk(q, s_ref, c, mx8):
        rows = chunk_rows(c)
        s = jax.lax.dot_general(kn_ref[rows, :], q, NT_DIMS,
                                preferred_element_type=F32)
        s_ref[rows, :] = s
        return jnp.maximum(mx8, jnp.max(s.reshape(fold), axis=0))

    def pv_chunk(s_ref, c, m8, l8):
        p3 = jnp.exp2(s_ref[chunk_rows(c), :].reshape(fold) - m8[None])
        p = p3.reshape(ATTN_TK, mq).astype(BF16)
        acc_ref[...] += jnp.dot(vt_ref[c], p, preferred_element_type=F32)
        return l8 + jnp.sum(p3, axis=0)

    def finalize(t, l8):
        o_t = acc_ref[...] / jnp.sum(l8, axis=0, keepdims=True)
        rows = tile_rows(t)
        for j in range(GROUP):
            o_ref[rows, j * HEAD_DIM:(j + 1) * HEAD_DIM] = (
                o_t[:, j * ATTN_TQ:(j + 1) * ATTN_TQ].T.astype(BF16))

    neg8 = jnp.full((SUBLANES, mq), -jnp.inf, F32)
    zero8 = jnp.zeros((SUBLANES, mq), F32)

    def scores_only(q_cur, q_next, s_ref):
        prep_q(1, q_next)
        q = q_cur[...]
        return col_max(jax.lax.fori_loop(
            0, n_chunks, lambda c, mx8: qk_chunk(q, s_ref, c, mx8), neg8,
            unroll=True))

    def values_only(t, s_ref, m8):
        acc_ref[...] = jnp.zeros_like(acc_ref)
        l8 = jax.lax.fori_loop(
            0, n_chunks, lambda c, l8: pv_chunk(s_ref, c, m8, l8), zero8,
            unroll=True)
        finalize(t, l8)

    def phase(t, q_cur, q_next, s_write, s_read, m8_prev):
        if q_next is not None:
            prep_q(t + 1, q_next)
        q = q_cur[...]
        acc_ref[...] = jnp.zeros_like(acc_ref)

        def body(c, carry):
            mx8, l8 = carry
            return qk_chunk(q, s_write, c, mx8), pv_chunk(s_read, c, m8_prev, l8)

        mx8, l8 = jax.lax.fori_loop(0, n_chunks, body, (neg8, zero8), unroll=True)
        finalize(t - 1, l8)
        return col_max(mx8)

    def phase_pair(i, m8):
        t = 2 * i + 1
        m8 = phase(t, qb_ref, qa_ref, sb_ref, sa_ref, m8)
        return phase(t + 1, qa_ref, qb_ref, sa_ref, sb_ref, m8)

    prep_q(0, qa_ref)
    m8 = scores_only(qa_ref, qb_ref, sa_ref)
    m8 = jax.lax.fori_loop(0, (n_tiles - 2) // 2, phase_pair, m8)
    m8 = phase(n_tiles - 1, qb_ref, None, sb_ref, sa_ref, m8)
    values_only(n_tiles - 1, sb_ref, m8)


def _attention(z2, q_gain, k_gain, cos, sin, layer, batch, seq):
    t = z2.shape[0]
    gw = GROUP * HEAD_DIM
    mq = GROUP * ATTN_TQ
    gain_spec = pl.BlockSpec((None, 1, HEAD_DIM), lambda b, g: (layer, 0, 0))
    table_spec = _resident((seq, HEAD_DIM), lambda b, g: (0, 0))
    return pl.pallas_call(
        _attn_kernel,
        grid=(batch, N_KV_HEADS),
        in_specs=[
            pl.BlockSpec((seq, gw), lambda b, g: (b, Z_Q // gw + g)),
            pl.BlockSpec((seq, HEAD_DIM), lambda b, g: (b, Z_K // HEAD_DIM + g)),
            pl.BlockSpec((seq, HEAD_DIM), lambda b, g: (b, Z_V // HEAD_DIM + g)),
            gain_spec, gain_spec, table_spec, table_spec,
        ],
        out_specs=pl.BlockSpec((seq, gw), lambda b, g: (b, g)),
        out_shape=jax.ShapeDtypeStruct((t, ATTN_WIDTH), BF16),
        scratch_shapes=[
            pltpu.VMEM((seq, mq), F32),
            pltpu.VMEM((seq, mq), F32),
            pltpu.VMEM((seq, HEAD_DIM), BF16),
            pltpu.VMEM((seq // ATTN_TK, HEAD_DIM, ATTN_TK), BF16),
            pltpu.VMEM((mq, HEAD_DIM), BF16),
            pltpu.VMEM((mq, HEAD_DIM), BF16),
            pltpu.VMEM((HEAD_DIM, mq), F32),
        ],
        compiler_params=_params(2),
        name="attention",
    )(z2, z2, z2, q_gain, k_gain, cos, sin)


def _gate_kernel(a_ref, o_ref, ga_ref, gb_ref, ba_ref, bb_ref, wc_ref, wa_ref, m_ref):
    ya = jnp.dot(a_ref[...], wc_ref[...], preferred_element_type=F32)
    yb = jnp.dot(o_ref[...], wa_ref[...], preferred_element_type=F32)
    gate_a = jax.nn.sigmoid(ga_ref[...].astype(F32) + ba_ref[...])
    gate_b = jax.nn.sigmoid(gb_ref[...].astype(F32) + bb_ref[...])
    m_ref[...] = (gate_a * ya + gate_b * yb).astype(BF16)


def _gated_branches(a2, o2, z2, bias_a, bias_b, w_conv_out, w_attn_out, layer,
                    tm=1024, tn=1024):
    t = a2.shape[0]
    row_spec = pl.BlockSpec((tm, D_MODEL), lambda i, j: (i, 0))
    zspec = lambda off: pl.BlockSpec((tm, tn), lambda i, j, off=off: (i, off // tn + j))
    bspec = pl.BlockSpec((None, 1, tn), lambda i, j: (layer, 0, j))
    wspec = pl.BlockSpec((None, D_MODEL, tn), lambda i, j: (layer, 0, j))
    return pl.pallas_call(
        _gate_kernel,
        grid=(t // tm, D_MODEL // tn),
        in_specs=[row_spec, row_spec, zspec(Z_GA), zspec(Z_GB), bspec, bspec,
                  wspec, wspec],
        out_specs=pl.BlockSpec((tm, tn), lambda i, j: (i, j)),
        out_shape=jax.ShapeDtypeStruct((t, D_MODEL), BF16),
        compiler_params=_params(2),
        name="gated_branches",
    )(a2, o2, z2, z2, bias_a, bias_b, w_conv_out, w_attn_out)


MERGE_TM = 512


def _proj_norm_res_kernel(m_ref, w_ref, g_ref, x_ref, o_ref):
    y = jnp.dot(m_ref[...], w_ref[...], preferred_element_type=F32)
    o_ref[...] = x_ref[...] + y * _rms_scale(y) * g_ref[...]


def _proj_norm_res(m2, w, gain, x2, layer, name, tm=ROW_TM):
    t, k = m2.shape
    xspec = pl.BlockSpec((tm, D_MODEL), lambda i: (i, 0))
    if w.ndim == 3:
        wspec = _resident((None, k, D_MODEL), lambda i: (layer, 0, 0))
    else:
        wspec = _resident((k, D_MODEL), lambda i: (0, 0))
    return pl.pallas_call(
        _proj_norm_res_kernel,
        grid=(t // tm,),
        in_specs=[
            pl.BlockSpec((tm, k), lambda i: (i, 0)),
            wspec,
            pl.BlockSpec((None, 1, D_MODEL), lambda i: (layer, 0, 0)),
            xspec,
        ],
        out_specs=xspec,
        out_shape=jax.ShapeDtypeStruct((t, D_MODEL), F32),
        compiler_params=_params(1),
        name=name,
    )(m2, w, gain, x2)


def _mlp_up_kernel(x_ref, g_ref, w_ref, a_ref):
    xf = x_ref[...]
    h = (xf * _rms_scale(xf) * g_ref[...]).astype(BF16)
    for n in range(D_FF // DOT_TN):
        cols = slice(n * DOT_TN, (n + 1) * DOT_TN)
        up = jnp.dot(h, w_ref[:, cols], preferred_element_type=F32)
        a_ref[:, cols] = jnp.square(jnp.maximum(up, 0.0)).astype(BF16)


def _mlp_up_cast_kernel(x_ref, g_ref, w_ref, w_next_ref, a_ref, w_next_b_ref):
    _mlp_up_kernel(x_ref, g_ref, w_ref, a_ref)
    w_next_b_ref[...] = w_next_ref[...].astype(BF16)


def _mlp_up(x2, g_pre, w_up_b, layer, w_in=None, tm=ROW_TM):
    t = x2.shape[0]
    steps = t // tm
    in_specs = [
        pl.BlockSpec((tm, D_MODEL), lambda i: (i, 0)),
        pl.BlockSpec((None, 1, D_MODEL), lambda i: (layer, 0, 0)),
        _resident((D_MODEL, D_FF), lambda i: (0, 0)),
    ]
    out_specs = [pl.BlockSpec((tm, D_FF), lambda i: (i, 0))]
    out_shape = [jax.ShapeDtypeStruct((t, D_FF), BF16)]
    args = [x2, g_pre, w_up_b]
    if w_in is not None:
        rows = D_MODEL // steps
        in_specs.append(pl.BlockSpec((None, rows, IN_WIDTH), lambda i: (layer + 1, i, 0)))
        out_specs.append(pl.BlockSpec((rows, IN_WIDTH), lambda i: (i, 0)))
        out_shape.append(jax.ShapeDtypeStruct((D_MODEL, IN_WIDTH), BF16))
        args.append(w_in)
    return pl.pallas_call(
        _mlp_up_kernel if w_in is None else _mlp_up_cast_kernel,
        grid=(steps,),
        in_specs=in_specs,
        out_specs=out_specs,
        out_shape=out_shape,
        compiler_params=_params(1),
        name="mlp_up",
    )(*args)


def _rope_tables(seq):
    pos = np.arange(seq)
    inv_freq = ROPE_THETA ** (-np.arange(0, AXIS_DIM, 2, dtype=np.float32) / AXIS_DIM)
    inv_freq = jnp.asarray(inv_freq, F32)
    row = jnp.asarray(pos // GRID_W, F32)[:, None] * inv_freq
    col = jnp.asarray(pos % GRID_W, F32)[:, None] * inv_freq
    cos = jnp.concatenate([jnp.cos(row), jnp.cos(row), jnp.cos(col), jnp.cos(col)], axis=1)
    sin = jnp.concatenate([-jnp.sin(row), jnp.sin(row), -jnp.sin(col), jnp.sin(col)], axis=1)
    return cos, sin


def kernel(x, norm_mix_pre, w_in, gate_bias, conv_w, q_norm, k_norm, w_out_conv, w_out_attn,
           w_merge, norm_mix_post, norm_mlp_pre, w_up, w_down, norm_mlp_post):
    b, s, d = x.shape
    depth = w_in.shape[0]
    t = b * s

    w_in_l = w_in[0].astype(BF16)
    w_conv_out_b = w_out_conv.astype(BF16)
    w_attn_out_b = w_out_attn.astype(BF16)
    w_merge_b = w_merge.astype(BF16)
    row3 = lambda p: p.reshape(depth, 1, p.shape[-1])
    bias_a = row3(gate_bias[:, :D_MODEL])
    bias_b = row3(gate_bias[:, D_MODEL:])
    cos, sin = _rope_tables(s)

    x2 = x.reshape(t, d)
    for l in range(depth):
        z2, w_up_l, w_down_l = _in_proj(x2, row3(norm_mix_pre), w_in_l, w_up, w_down, l)
        a3 = _conv_mix(z2.reshape(b, s, IN_WIDTH), conv_w, l)
        o2 = _attention(z2, row3(q_norm), row3(k_norm), cos, sin, l, b, s)
        m2 = _gated_branches(a3.reshape(t, CONV_WIDTH), o2, z2, bias_a, bias_b,
                             w_conv_out_b, w_attn_out_b, l)
        x2 = _proj_norm_res(m2, w_merge_b, row3(norm_mix_post), x2, l, "merge_proj",
                            tm=MERGE_TM)
        if l + 1 < depth:
            act, w_in_l = _mlp_up(x2, row3(norm_mlp_pre), w_up_l, l, w_in)
        else:
            (act,) = _mlp_up(x2, row3(norm_mlp_pre), w_up_l, l)
        x2 = _proj_norm_res(act, w_down_l, row3(norm_mlp_post), x2, l, "mlp_down")
    return x2.reshape(b, s, d)
```

```python
import math

import jax
import jax.numpy as jnp
import numpy as np
from jax.experimental import pallas as pl
from jax.experimental.pallas import tpu as pltpu

D_MODEL = 2048
GRID_W = 64
HEAD_DIM = 128
N_Q_HEADS = 16
N_KV_HEADS = 4
GROUP = N_Q_HEADS // N_KV_HEADS
ATTN_WIDTH = N_Q_HEADS * HEAD_DIM
KV_WIDTH = N_KV_HEADS * HEAD_DIM
CONV_WIDTH = D_MODEL
D_FF = 4 * D_MODEL
ROPE_THETA = 10000.0
RMS_EPS = 1e-6
AXIS_DIM = HEAD_DIM // 2
N_FREQ = AXIS_DIM // 2

IN_WIDTH = 3 * CONV_WIDTH + ATTN_WIDTH + 2 * KV_WIDTH + 2 * D_MODEL
Z_B, Z_C, Z_IN = 0, CONV_WIDTH, 2 * CONV_WIDTH
Z_Q = 3 * CONV_WIDTH
Z_K = Z_Q + ATTN_WIDTH
Z_V = Z_K + KV_WIDTH
Z_GA = Z_V + KV_WIDTH
Z_GB = Z_GA + D_MODEL

V7X_VMEM_BYTES = 64 * 1024 * 1024
VMEM_LIMIT = V7X_VMEM_BYTES - 8 * 1024 * 1024
SUBLANES = 8

BF16 = jnp.bfloat16
F32 = jnp.float32


def _params(n_axes):
    return pltpu.CompilerParams(
        dimension_semantics=("arbitrary",) * n_axes, vmem_limit_bytes=VMEM_LIMIT)


def _resident(block_shape, index_map):
    return pl.BlockSpec(block_shape, index_map, pipeline_mode=pl.Buffered(1))


def _rms_scale(xf):
    return jax.lax.rsqrt(jnp.mean(xf * xf, axis=-1, keepdims=True) + RMS_EPS)


ROW_TM = 256
DOT_TN = 1024
IN_PARTS = 2
IN_PART_WIDTH = IN_WIDTH // IN_PARTS
IN_DOT_TN = IN_PART_WIDTH // 2


def _in_proj_kernel(x_ref, g_ref, w_ref, wu_ref, wd_ref, z_ref, wub_ref, wdb_ref):
    xf = x_ref[...]
    h = (xf * _rms_scale(xf) * g_ref[...]).astype(BF16)
    for n in range(IN_PART_WIDTH // IN_DOT_TN):
        cols = slice(n * IN_DOT_TN, (n + 1) * IN_DOT_TN)
        z_ref[:, cols] = jnp.dot(h, w_ref[:, cols],
                                 preferred_element_type=F32).astype(BF16)
    wub_ref[...] = wu_ref[...].astype(BF16)
    wdb_ref[...] = wd_ref[...].astype(BF16)


def _in_proj(x2, gain, w_in_b, w_up, w_down, layer, tm=ROW_TM):
    t = x2.shape[0]
    n_rows = t // tm
    steps = IN_PARTS * n_rows
    up_rows = D_MODEL // steps
    down_rows = D_FF // steps
    slab = lambda p, i: p * n_rows + i
    return pl.pallas_call(
        _in_proj_kernel,
        grid=(IN_PARTS, n_rows),
        in_specs=[
            pl.BlockSpec((tm, D_MODEL), lambda p, i: (i, 0)),
            pl.BlockSpec((None, 1, D_MODEL), lambda p, i: (layer, 0, 0)),
            _resident((D_MODEL, IN_PART_WIDTH), lambda p, i: (0, p)),
            pl.BlockSpec((None, up_rows, D_FF), lambda p, i: (layer, slab(p, i), 0)),
            pl.BlockSpec((None, down_rows, D_MODEL), lambda p, i: (layer, slab(p, i), 0)),
        ],
        out_specs=[
            pl.BlockSpec((tm, IN_PART_WIDTH), lambda p, i: (i, p)),
            pl.BlockSpec((up_rows, D_FF), lambda p, i: (slab(p, i), 0)),
            pl.BlockSpec((down_rows, D_MODEL), lambda p, i: (slab(p, i), 0)),
        ],
        out_shape=[
            jax.ShapeDtypeStruct((t, IN_WIDTH), BF16),
            jax.ShapeDtypeStruct((D_MODEL, D_FF), BF16),
            jax.ShapeDtypeStruct((D_FF, D_MODEL), BF16),
        ],
        compiler_params=_params(2),
        name="in_proj",
    )(x2, gain, w_in_b, w_up, w_down)


def _conv_kernel(b_ref, c_ref, i_ref, w_ref, o_ref):
    s = b_ref.shape[0]
    u = c_ref[...].astype(F32) * i_ref[...].astype(F32)
    row = jax.lax.broadcasted_iota(jnp.int32, u.shape, 0)
    prev = jnp.where(row == 0, 0.0, pltpu.roll(u, 1, 0))
    nxt = jnp.where(row == s - 1, 0.0, pltpu.roll(u, s - 1, 0))
    w = w_ref[...]
    conv = w[0:1] * prev + w[1:2] * u + w[2:3] * nxt
    o_ref[...] = (b_ref[...].astype(F32) * conv).astype(BF16)


def _conv_mix(z3, conv_w, layer, tc=256):
    b, s, _ = z3.shape
    zspec = lambda off: pl.BlockSpec(
        (None, s, tc), lambda bi, j, off=off: (bi, 0, off // tc + j))
    return pl.pallas_call(
        _conv_kernel,
        grid=(b, CONV_WIDTH // tc),
        in_specs=[
            zspec(Z_B), zspec(Z_C), zspec(Z_IN),
            pl.BlockSpec((None, 3, tc), lambda bi, j: (layer, 0, j)),
        ],
        out_specs=pl.BlockSpec((None, s, tc), lambda bi, j: (bi, 0, j)),
        out_shape=jax.ShapeDtypeStruct((b, s, CONV_WIDTH), BF16),
        compiler_params=_params(2),
        name="conv_mix",
    )(z3, z3, z3, conv_w)


ATTN_TQ = 128
ATTN_TK = 512
NT_DIMS = (((1,), (1,)), ((), ()))
Q_SCALE = math.log2(math.e) / math.sqrt(HEAD_DIM)


def _rope_head(x, gain, cos, sin_signed):
    lane = jax.lax.broadcasted_iota(jnp.int32, x.shape, 1)
    y = x * _rms_scale(x) * gain
    first_half = (lane % AXIS_DIM) < N_FREQ
    partner = jnp.where(first_half,
                        pltpu.roll(y, HEAD_DIM - N_FREQ, 1),
                        pltpu.roll(y, N_FREQ, 1))
    return y * cos + partner * sin_signed


def _attn_kernel(q_ref, k_ref, v_ref, qg_ref, kg_ref, cos_ref, sin_ref, o_ref,
                 sa_ref, sb_ref, kn_ref, vt_ref, qa_ref, qb_ref, acc_ref):
    seq = q_ref.shape[0]
    n_tiles = seq // ATTN_TQ
    n_chunks = seq // ATTN_TK
    assert n_tiles % 2 == 0 and n_tiles >= 2
    mq = GROUP * ATTN_TQ
    fold = (ATTN_TK // SUBLANES, SUBLANES, mq)

    def tile_rows(t):
        return pl.ds(pl.multiple_of(t * ATTN_TQ, ATTN_TQ), ATTN_TQ)

    def chunk_rows(c):
        return pl.ds(pl.multiple_of(c * ATTN_TK, ATTN_TK), ATTN_TK)

    for c in range(n_chunks):
        vt_ref[c] = v_ref[c * ATTN_TK:(c + 1) * ATTN_TK, :].T

    def prep_k(c, carry):
        rows = chunk_rows(c)
        kn_ref[rows, :] = _rope_head(k_ref[rows, :].astype(F32), kg_ref[...],
                                     cos_ref[rows, :], sin_ref[rows, :]).astype(BF16)
        return carry

    jax.lax.fori_loop(0, n_chunks, prep_k, 0)

    def prep_q(t, dst_ref):
        rows = tile_rows(t)
        cos = cos_ref[rows, :]
        sin = sin_ref[rows, :]
        for j in range(GROUP):
            x = q_ref[rows, j * HEAD_DIM:(j + 1) * HEAD_DIM].astype(F32)
            y = _rope_head(x, qg_ref[...], cos, sin) * Q_SCALE
            dst_ref[j * ATTN_TQ:(j + 1) * ATTN_TQ, :] = y.astype(BF16)

    def col_max(mx8):
        return jnp.broadcast_to(jnp.max(mx8, axis=0, keepdims=True), mx8.shape)

    def qk_chunk(q, s_ref, c, mx8):
        rows = chunk_rows(c)
        s = jax.lax.dot_general(kn_ref[rows, :], q, NT_DIMS,
                                preferred_element_type=F32)
        s_ref[rows, :] = s
        return jnp.maximum(mx8, jnp.max(s.reshape(fold), axis=0))

    def pv_chunk(s_ref, c, m8, l8):
        p3 = jnp.exp2(s_ref[chunk_rows(c), :].reshape(fold) - m8[None])
        p = p3.reshape(ATTN_TK, mq).astype(BF16)
        acc_ref[...] += jnp.dot(vt_ref[c], p, preferred_element_type=F32)
        return l8 + jnp.sum(p3, axis=0)

    def finalize(t, l8):
        o_t = acc_ref[...] / jnp.sum(l8, axis=0, keepdims=True)
        rows = tile_rows(t)
        for j in range(GROUP):
            o_ref[rows, j * HEAD_DIM:(j + 1) * HEAD_DIM] = (
                o_t[:, j * ATTN_TQ:(j + 1) * ATTN_TQ].T.astype(BF16))

    neg8 = jnp.full((SUBLANES, mq), -jnp.inf, F32)
    zero8 = jnp.zeros((SUBLANES, mq), F32)

    def scores_only(q_cur, q_next, s_ref):
        prep_q(1, q_next)
        q = q_cur[...]
        return col_max(jax.lax.fori_loop(
            0, n_chunks, lambda c, mx8: qk_chunk(q, s_ref, c, mx8), neg8,
            unroll=True))

    def values_only(t, s_ref, m8):
        acc_ref[...] = jnp.zeros_like(acc_ref)
        l8 = jax.lax.fori_loop(
            0, n_chunks, lambda c, l8: pv_chunk(s_ref, c, m8, l8), zero8,
            unroll=True)
        finalize(t, l8)

    def phase(t, q_cur, q_next, s_write, s_read, m8_prev):
        if q_next is not None:
            prep_q(t + 1, q_next)
        q = q_cur[...]
        acc_ref[...] = jnp.zeros_like(acc_ref)

        def body(c, carry):
            mx8, l8 = carry
            return qk_chunk(q, s_write, c, mx8), pv_chunk(s_read, c, m8_prev, l8)

        mx8, l8 = jax.lax.fori_loop(0, n_chunks, body, (neg8, zero8), unroll=True)
        finalize(t - 1, l8)
        return col_max(mx8)

    def phase_pair(i, m8):
        t = 2 * i + 1
        m8 = phase(t, qb_ref, qa_ref, sb_ref, sa_ref, m8)
        return phase(t + 1, qa_ref, qb_ref, sa_ref, sb_ref, m8)

    prep_q(0, qa_ref)
    m8 = scores_only(qa_ref, qb_ref, sa_ref)
    m8 = jax.lax.fori_loop(0, (n_tiles - 2) // 2, phase_pair, m8)
    m8 = phase(n_tiles - 1, qb_ref, None, sb_ref, sa_ref, m8)
    values_only(n_tiles - 1, sb_ref, m8)


def _attention(z2, q_gain, k_gain, cos, sin, layer, batch, seq):
    t = z2.shape[0]
    gw = GROUP * HEAD_DIM
    mq = GROUP * ATTN_TQ
    gain_spec = pl.BlockSpec((None, 1, HEAD_DIM), lambda b, g: (layer, 0, 0))
    table_spec = _resident((seq, HEAD_DIM), lambda b, g: (0, 0))
    return pl.pallas_call(
        _attn_kernel,
        grid=(batch, N_KV_HEADS),
        in_specs=[
            pl.BlockSpec((seq, gw), lambda b, g: (b, Z_Q // gw + g)),
            pl.BlockSpec((seq, HEAD_DIM), lambda b, g: (b, Z_K // HEAD_DIM + g)),
            pl.BlockSpec((seq, HEAD_DIM), lambda b, g: (b, Z_V // HEAD_DIM + g)),
            gain_spec, gain_spec, table_spec, table_spec,
        ],
        out_specs=pl.BlockSpec((seq, gw), lambda b, g: (b, g)),
        out_shape=jax.ShapeDtypeStruct((t, ATTN_WIDTH), BF16),
        scratch_shapes=[
            pltpu.VMEM((seq, mq), F32),
            pltpu.VMEM((seq, mq), F32),
            pltpu.VMEM((seq, HEAD_DIM), BF16),
            pltpu.VMEM((seq // ATTN_TK, HEAD_DIM, ATTN_TK), BF16),
            pltpu.VMEM((mq, HEAD_DIM), BF16),
            pltpu.VMEM((mq, HEAD_DIM), BF16),
            pltpu.VMEM((HEAD_DIM, mq), F32),
        ],
        compiler_params=_params(2),
        name="attention",
    )(z2, z2, z2, q_gain, k_gain, cos, sin)


def _gate_kernel(a_ref, o_ref, ga_ref, gb_ref, ba_ref, bb_ref, wc_ref, wa_ref, m_ref):
    ya = jnp.dot(a_ref[...], wc_ref[...], preferred_element_type=F32)
    yb = jnp.dot(o_ref[...], wa_ref[...], preferred_element_type=F32)
    gate_a = jax.nn.sigmoid(ga_ref[...].astype(F32) + ba_ref[...])
    gate_b = jax.nn.sigmoid(gb_ref[...].astype(F32) + bb_ref[...])
    m_ref[...] = (gate_a * ya + gate_b * yb).astype(BF16)


def _gated_branches(a2, o2, z2, bias_a, bias_b, w_conv_out, w_attn_out, layer,
                    tm=1024, tn=1024):
    t = a2.shape[0]
    row_spec = pl.BlockSpec((tm, D_MODEL), lambda i, j: (i, 0))
    zspec = lambda off: pl.BlockSpec((tm, tn), lambda i, j, off=off: (i, off // tn + j))
    bspec = pl.BlockSpec((None, 1, tn), lambda i, j: (layer, 0, j))
    wspec = pl.BlockSpec((D_MODEL, tn), lambda i, j: (0, j))
    return pl.pallas_call(
        _gate_kernel,
        grid=(t // tm, D_MODEL // tn),
        in_specs=[row_spec, row_spec, zspec(Z_GA), zspec(Z_GB), bspec, bspec,
                  wspec, wspec],
        out_specs=pl.BlockSpec((tm, tn), lambda i, j: (i, j)),
        out_shape=jax.ShapeDtypeStruct((t, D_MODEL), BF16),
        compiler_params=_params(2),
        name="gated_branches",
    )(a2, o2, z2, z2, bias_a, bias_b, w_conv_out, w_attn_out)


MERGE_TM = 512


def _proj_norm_res_kernel(m_ref, w_ref, g_ref, x_ref, o_ref):
    y = jnp.dot(m_ref[...], w_ref[...], preferred_element_type=F32)
    o_ref[...] = x_ref[...] + y * _rms_scale(y) * g_ref[...]


def _proj_norm_res_cast_kernel(m_ref, w_ref, g_ref, x_ref, c0_ref, c1_ref, c2_ref,
                               o_ref, b0_ref, b1_ref, b2_ref):
    _proj_norm_res_kernel(m_ref, w_ref, g_ref, x_ref, o_ref)
    b0_ref[...] = c0_ref[...].astype(BF16)
    b1_ref[...] = c1_ref[...].astype(BF16)
    b2_ref[...] = c2_ref[...].astype(BF16)


def _proj_norm_res(m2, w, gain, x2, layer, name, tm=ROW_TM, cast_next=()):
    t, k = m2.shape
    steps = t // tm
    xspec = pl.BlockSpec((tm, D_MODEL), lambda i: (i, 0))
    in_specs = [
        pl.BlockSpec((tm, k), lambda i: (i, 0)),
        _resident((k, D_MODEL), lambda i: (0, 0)),
        pl.BlockSpec((None, 1, D_MODEL), lambda i: (layer, 0, 0)),
        xspec,
    ]
    out_specs = [xspec]
    out_shape = [jax.ShapeDtypeStruct((t, D_MODEL), F32)]
    body = _proj_norm_res_kernel
    if cast_next:
        rows = D_MODEL // steps
        body = _proj_norm_res_cast_kernel
        for _ in cast_next:
            in_specs.append(
                pl.BlockSpec((None, rows, D_MODEL), lambda i: (layer + 1, i, 0)))
            out_specs.append(pl.BlockSpec((rows, D_MODEL), lambda i: (i, 0)))
            out_shape.append(jax.ShapeDtypeStruct((D_MODEL, D_MODEL), BF16))
    return pl.pallas_call(
        body,
        grid=(steps,),
        in_specs=in_specs,
        out_specs=out_specs,
        out_shape=out_shape,
        compiler_params=_params(1),
        name=name,
    )(m2, w, gain, x2, *cast_next)


def _mlp_up_kernel(x_ref, g_ref, w_ref, a_ref):
    xf = x_ref[...]
    h = (xf * _rms_scale(xf) * g_ref[...]).astype(BF16)
    for n in range(D_FF // DOT_TN):
        cols = slice(n * DOT_TN, (n + 1) * DOT_TN)
        up = jnp.dot(h, w_ref[:, cols], preferred_element_type=F32)
        a_ref[:, cols] = jnp.square(jnp.maximum(up, 0.0)).astype(BF16)


def _mlp_up_cast_kernel(x_ref, g_ref, w_ref, w_next_ref, a_ref, w_next_b_ref):
    _mlp_up_kernel(x_ref, g_ref, w_ref, a_ref)
    w_next_b_ref[...] = w_next_ref[...].astype(BF16)


def _mlp_up(x2, g_pre, w_up_b, layer, w_in=None, tm=ROW_TM):
    t = x2.shape[0]
    steps = t // tm
    in_specs = [
        pl.BlockSpec((tm, D_MODEL), lambda i: (i, 0)),
        pl.BlockSpec((None, 1, D_MODEL), lambda i: (layer, 0, 0)),
        _resident((D_MODEL, D_FF), lambda i: (0, 0)),
    ]
    out_specs = [pl.BlockSpec((tm, D_FF), lambda i: (i, 0))]
    out_shape = [jax.ShapeDtypeStruct((t, D_FF), BF16)]
    args = [x2, g_pre, w_up_b]
    if w_in is not None:
        rows = D_MODEL // steps
        in_specs.append(pl.BlockSpec((None, rows, IN_WIDTH), lambda i: (layer + 1, i, 0)))
        out_specs.append(pl.BlockSpec((rows, IN_WIDTH), lambda i: (i, 0)))
        out_shape.append(jax.ShapeDtypeStruct((D_MODEL, IN_WIDTH), BF16))
        args.append(w_in)
    return pl.pallas_call(
        _mlp_up_kernel if w_in is None else _mlp_up_cast_kernel,
        grid=(steps,),
        in_specs=in_specs,
        out_specs=out_specs,
        out_shape=out_shape,
        compiler_params=_params(1),
        name="mlp_up",
    )(*args)


def _rope_tables(seq):
    pos = np.arange(seq)
    inv_freq = ROPE_THETA ** (-np.arange(0, AXIS_DIM, 2, dtype=np.float32) / AXIS_DIM)
    inv_freq = jnp.asarray(inv_freq, F32)
    row = jnp.asarray(pos // GRID_W, F32)[:, None] * inv_freq
    col = jnp.asarray(pos % GRID_W, F32)[:, None] * inv_freq
    cos = jnp.concatenate([jnp.cos(row), jnp.cos(row), jnp.cos(col), jnp.cos(col)], axis=1)
    sin = jnp.concatenate([-jnp.sin(row), jnp.sin(row), -jnp.sin(col), jnp.sin(col)], axis=1)
    return cos, sin


def kernel(x, norm_mix_pre, w_in, gate_bias, conv_w, q_norm, k_norm, w_out_conv, w_out_attn,
           w_merge, norm_mix_post, norm_mlp_pre, w_up, w_down, norm_mlp_post):
    b, s, d = x.shape
    depth = w_in.shape[0]
    t = b * s

    w_in_l = w_in[0].astype(BF16)
    w_conv_out_l = w_out_conv[0].astype(BF16)
    w_attn_out_l = w_out_attn[0].astype(BF16)
    w_merge_l = w_merge[0].astype(BF16)
    row3 = lambda p: p.reshape(depth, 1, p.shape[-1])
    bias_a = row3(gate_bias[:, :D_MODEL])
    bias_b = row3(gate_bias[:, D_MODEL:])
    cos, sin = _rope_tables(s)

    x2 = x.reshape(t, d)
    for l in range(depth):
        z2, w_up_l, w_down_l = _in_proj(x2, row3(norm_mix_pre), w_in_l, w_up, w_down, l)
        a3 = _conv_mix(z2.reshape(b, s, IN_WIDTH), conv_w, l)
        o2 = _attention(z2, row3(q_norm), row3(k_norm), cos, sin, l, b, s)
        m2 = _gated_branches(a3.reshape(t, CONV_WIDTH), o2, z2, bias_a, bias_b,
                             w_conv_out_l, w_attn_out_l, l)
        if l + 1 < depth:
            x2, w_conv_out_l, w_attn_out_l, w_merge_l = _proj_norm_res(
                m2, w_merge_l, row3(norm_mix_post), x2, l, "merge_proj", tm=MERGE_TM,
                cast_next=(w_out_conv, w_out_attn, w_merge))
            act, w_in_l = _mlp_up(x2, row3(norm_mlp_pre), w_up_l, l, w_in)
        else:
            (x2,) = _proj_norm_res(m2, w_merge_l, row3(norm_mix_post), x2, l,
                                   "merge_proj", tm=MERGE_TM)
            (act,) = _mlp_up(x2, row3(norm_mlp_pre), w_up_l, l)
        (x2,) = _proj_norm_res(act, w_down_l, row3(norm_mlp_post), x2, l, "mlp_down")
    return x2.reshape(b, s, d)
```
